```python
import jax, jax.numpy as jnp
from jax import lax
import numpy as np

D_MODEL = 1024
BATCH = 16
SEQ = 2048
DEPTH = 1

D_PLE = 256
GLA_HEADS = 4
GLA_DK = 64
GLA_DV = 128
GLA_KEY = GLA_HEADS * GLA_DK
GLA_VAL = GLA_HEADS * GLA_DV
GLA_GATE_RANK = 16
GLA_GATE_NORM = 16.0
GLA_CHUNK = 64
POOL_WINDOWS = (2, 4, 8, 16)
POOL_GROUPS = len(POOL_WINDOWS)
POOL_WIDTH = D_MODEL - GLA_VAL
POOL_GC = POOL_WIDTH // POOL_GROUPS
D_MIX = GLA_VAL + POOL_WIDTH
D_IN = 2 * GLA_KEY + 2 * GLA_VAL + GLA_GATE_RANK + POOL_WIDTH
D_FF = ((8 * D_MODEL + 3 * 256 - 1) // (3 * 256)) * 256
EPS = 1e-6

kernel_name = "hybrid_gla_multiscale_pool_block"


def rmsnorm(x, g):
    xf = x.astype(jnp.float32)
    y = xf * lax.rsqrt(jnp.mean(xf * xf, axis=-1, keepdims=True) + EPS)
    return (y * g.astype(jnp.float32)).astype(x.dtype)


def gla_chunked(q, k, v, gk):
    B, T, H, DK = q.shape
    DV = v.shape[-1]
    C = GLA_CHUNK
    N = T // C

    def to_chunks(a):
        return a.astype(jnp.float32).reshape(B, N, C, H, a.shape[-1]).transpose(1, 0, 3, 2, 4)

    qc = to_chunks(q) * (DK ** -0.5)
    kc = to_chunks(k)
    vc = to_chunks(v)
    bc = jnp.cumsum(to_chunks(gk), axis=3)
    mask = jnp.tril(jnp.ones((C, C), dtype=bool))[:, :, None]

    def step(S, inp):
        qi, ki, vi, bi = inp
        diff = bi[:, :, :, None, :] - bi[:, :, None, :, :]
        decay = jnp.exp(jnp.where(mask, diff, -jnp.inf))
        attn = jnp.einsum('bhid,bhjd,bhijd->bhij', qi, ki, decay)
        o = jnp.einsum('bhij,bhjv->bhiv', attn, vi) + \
            jnp.einsum('bhid,bhdv->bhiv', qi * jnp.exp(bi), S)
        b_last = bi[:, :, -1, :]
        S = jnp.exp(b_last)[..., None] * S + jnp.einsum(
            'bhjd,bhjv->bhdv', ki * jnp.exp(b_last[:, :, None, :] - bi), vi)
        return S, o

    S0 = jnp.zeros((B, H, DK, DV), jnp.float32)
    _, o = lax.scan(step, S0, (qc, kc, vc, bc))
    return o.transpose(1, 0, 3, 2, 4).reshape(B, T, H, DV)


def multiscale_pool(u, w_pool, pool_scale):
    B, T, _ = u.shape
    uf = u.astype(jnp.float32).reshape(B, T, POOL_GROUPS, POOL_GC)
    cs = jnp.cumsum(uf, axis=1)
    t = jnp.arange(T)
    outs = []
    for gi, w in enumerate(POOL_WINDOWS):
        c = cs[:, :, gi]
        prev = jnp.pad(c, ((0, 0), (w, 0), (0, 0)))[:, :T]
        cnt = jnp.minimum(t + 1, w).astype(jnp.float32)[None, :, None]
        outs.append((c - prev) / cnt - uf[:, :, gi])
    pooled = jnp.stack(outs, axis=2)
    y = jnp.einsum('btgc,gcd->btgd', pooled, w_pool.astype(jnp.float32))
    y = y.reshape(B, T, POOL_WIDTH) * pool_scale.astype(jnp.float32)
    return y.astype(u.dtype)


def setup_inputs(seed: int = 0) -> dict:
    key = jax.random.key(seed)
    ks = jax.random.split(key, 20)
    f32 = jnp.float32

    def nrm(k, shape, fan_in):
        return jax.random.normal(k, shape, f32) * (fan_in ** -0.5)

    def gain(k, shape):
        return 1.0 + 0.02 * jax.random.normal(k, shape, f32)

    return {
        "x": jax.random.normal(ks[0], (BATCH, SEQ, D_MODEL), f32),
        "p": jax.random.normal(ks[1], (DEPTH, BATCH, SEQ, D_PLE), f32),
        "ln_mix": gain(ks[2], (DEPTH, D_MODEL)),
        "w_in": nrm(ks[3], (DEPTH, D_MODEL, D_IN), D_MODEL),
        "w_gate_lr": nrm(ks[4], (DEPTH, GLA_GATE_RANK, GLA_KEY), GLA_GATE_RANK),
        "b_gate": 0.1 * jax.random.normal(ks[5], (DEPTH, GLA_KEY), f32),
        "gla_norm": gain(ks[6], (DEPTH, GLA_DV)),
        "w_pool": nrm(ks[7], (DEPTH, POOL_GROUPS, POOL_GC, POOL_GC), POOL_GC),
        "pool_scale": gain(ks[8], (DEPTH, POOL_WIDTH)),
        "w_out": nrm(ks[9], (DEPTH, D_MIX, D_MODEL), D_MIX),
        "ln_ffn": gain(ks[10], (DEPTH, D_MODEL)),
        "w_ffn_gate": nrm(ks[11], (DEPTH, D_MODEL, D_FF), D_MODEL),
        "w_ffn_up": nrm(ks[12], (DEPTH, D_MODEL, D_FF), D_MODEL),
        "w_ffn_down": nrm(ks[13], (DEPTH, D_FF, D_MODEL), D_FF),
        "ln_ple": gain(ks[14], (DEPTH, D_MODEL)),
        "w_ple_gate": nrm(ks[15], (DEPTH, D_MODEL, D_MODEL), D_MODEL),
        "w_ple_proj": nrm(ks[16], (DEPTH, D_PLE, D_MODEL), D_PLE),
        "ln_final": gain(ks[17], (D_MODEL,)),
    }


def reference(x, p, ln_mix, w_in, w_gate_lr, b_gate, gla_norm, w_pool, pool_scale,
              w_out, ln_ffn, w_ffn_gate, w_ffn_up, w_ffn_down, ln_ple, w_ple_gate,
              w_ple_proj, ln_final):
    B, T, _ = x.shape
    splits = np.cumsum([GLA_KEY, GLA_KEY, GLA_VAL, GLA_VAL, GLA_GATE_RANK]).tolist()
    h = x
    for i in range(DEPTH):
        a = rmsnorm(h, ln_mix[i])
        z = a @ w_in[i]
        q, k, v, g_out, g_lr, u = jnp.split(z, splits, axis=-1)
        gk = jax.nn.log_sigmoid((g_lr @ w_gate_lr[i] + b_gate[i]).astype(jnp.float32)) / GLA_GATE_NORM
        o = gla_chunked(q.reshape(B, T, GLA_HEADS, GLA_DK),
                        k.reshape(B, T, GLA_HEADS, GLA_DK),
                        v.reshape(B, T, GLA_HEADS, GLA_DV),
                        gk.reshape(B, T, GLA_HEADS, GLA_DK))
        o = rmsnorm(o, gla_norm[i]).reshape(B, T, GLA_VAL).astype(h.dtype)
        o = o * jax.nn.silu(g_out)
        y_pool = multiscale_pool(u, w_pool[i], pool_scale[i])
        mix = jnp.concatenate([o, y_pool], axis=-1) @ w_out[i]
        h = h + mix
        f = rmsnorm(h, ln_ffn[i])
        h = h + (jax.nn.silu(f @ w_ffn_gate[i]) * (f @ w_ffn_up[i])) @ w_ffn_down[i]
        e = rmsnorm(h, ln_ple[i])
        h = h + jax.nn.sigmoid(e @ w_ple_gate[i]) * (p[i].astype(h.dtype) @ w_ple_proj[i])
    return rmsnorm(h, ln_final)
```

```python
import functools

import jax
import jax.numpy as jnp
from jax import lax
from jax.experimental import pallas as pl
from jax.experimental.pallas import tpu as pltpu

D_MODEL = 1024
D_PLE = 256
GLA_HEADS = 4
GLA_DK = 64
GLA_DV = 128
GLA_KEY = GLA_HEADS * GLA_DK
GLA_VAL = GLA_HEADS * GLA_DV
GLA_GATE_RANK = 16
GLA_GATE_NORM = 16.0
GLA_CHUNK = 64
POOL_WINDOWS = (2, 4, 8, 16)
POOL_GC = 128
POOL_WIDTH = 512
D_FF = 2816
EPS = 1e-6

LANES = 128
SUB_BLOCK = 16
N_SUB = GLA_CHUNK // SUB_BLOCK
POOL_HALO = max(POOL_WINDOWS)
FF_CHUNK = 256
VMEM_LIMIT = 56 * 1024 * 1024

F32 = jnp.float32
BF16 = jnp.bfloat16
NT = (((1,), (1,)), ((), ()))
TN = (((0,), (0,)), ((), ()))


def _dot(a, b, dims=None):
    if dims is None:
        return jnp.dot(a, b, preferred_element_type=F32)
    return lax.dot_general(a, b, dims, preferred_element_type=F32)


def _rmsnorm(x, g):
    return x * lax.rsqrt(jnp.mean(x * x, axis=-1, keepdims=True) + EPS) * g


def _const_spec(shape):
    nd = len(shape)
    return pl.BlockSpec(shape, lambda *_: (0,) * nd, pipeline_mode=pl.Buffered(1))


def _inproj_kernel(x_ref, ln_ref, wmain_ref, wglr_ref, wgate_ref, bgate_ref,
                   qkg_ref, vgu_ref):
    a = _rmsnorm(x_ref[...], ln_ref[...]).astype(BF16)
    z = _dot(a, wmain_ref[...])
    g_lr = _dot(a, wglr_ref[...])
    gate = _dot(g_lr.astype(BF16), wgate_ref[...]) + bgate_ref[...]
    gk = (jnp.minimum(gate, 0.0) - jnp.log1p(jnp.exp(-jnp.abs(gate)))) * (1.0 / GLA_GATE_NORM)
    qkg_ref[:, 0:2 * GLA_KEY] = z[:, 0:2 * GLA_KEY]
    qkg_ref[:, 2 * GLA_KEY:3 * GLA_KEY] = gk
    vgu_ref[...] = z[:, 2 * GLA_KEY:]


def _mix_kernel(qkg_ref, vgu_ref, x_ref, tril_ref, esel_ref, gnorm_ref, wpool_ref,
                pscale_ref, wout_ref, h_ref, state_ref, ubuf_ref, w_ref, o_ref, *, tm):
    t = pl.program_id(1)

    @pl.when(t == 0)
    def _():
        state_ref[...] = jnp.zeros_like(state_ref)
        ubuf_ref[0:POOL_HALO, :] = jnp.zeros((POOL_HALO, POOL_WIDTH), F32)

    q = qkg_ref[:, 0:GLA_KEY] * (GLA_DK ** -0.5)
    k = qkg_ref[:, GLA_KEY:2 * GLA_KEY]
    gk = qkg_ref[:, 2 * GLA_KEY:3 * GLA_KEY]
    v = vgu_ref[:, 0:GLA_VAL].astype(BF16)

    g_hi = gk.astype(BF16)
    rem = gk - g_hi.astype(F32)
    g_mid = rem.astype(BF16)
    g_lo = (rem - g_mid.astype(F32)).astype(BF16)
    tril = tril_ref[...]
    b = _dot(tril, g_hi) + _dot(tril, g_mid) + _dot(tril, g_lo)

    n_grp = tm // SUB_BLOCK
    b3 = b.reshape(n_grp, SUB_BLOCK, GLA_KEY)
    k3 = k.reshape(n_grp, SUB_BLOCK, GLA_KEY)

    def group_row(x3, j):
        return jnp.broadcast_to(x3[:, j:j + 1, :], x3.shape).reshape(tm, GLA_KEY)

    row_in_blk = lax.broadcasted_iota(jnp.int32, (tm, GLA_KEY), 0) % SUB_BLOCK
    for j in range(SUB_BLOCK):
        expo = jnp.where(row_in_blk >= j, b - group_row(b3, j), -jnp.inf)
        w_ref[:, j * GLA_KEY:(j + 1) * GLA_KEY] = (
            q * group_row(k3, j) * jnp.exp(expo)).astype(BF16)
    a_diag = _dot(w_ref[...], esel_ref[...])

    q_off = (q * jnp.exp(b - group_row(b3, 0))).astype(BF16)

    rows = lax.broadcasted_iota(jnp.int32, (GLA_CHUNK, GLA_KEY), 0) // SUB_BLOCK
    lanes = (lax.broadcasted_iota(jnp.int32, (GLA_CHUNK, GLA_KEY), 1) % GLA_CHUNK) // SUB_BLOCK
    diag_mask = rows == lanes
    n_cat = SUB_BLOCK * (N_SUB * (N_SUB - 1) // 2)
    prow = lax.broadcasted_iota(jnp.int32, (GLA_CHUNK, n_cat), 0) // SUB_BLOCK
    pcol = lax.broadcasted_iota(jnp.int32, (GLA_CHUNK, n_cat), 1)
    pgrp = jnp.zeros_like(pcol)
    start = 0
    for i in range(1, N_SUB):
        pgrp = jnp.where((pcol >= start) & (pcol < start + i * SUB_BLOCK), i, pgrp)
        start += i * SUB_BLOCK
    off_mask = prow == pgrp

    for c in range(tm // GLA_CHUNK):
        r0 = c * GLA_CHUNK
        bc = b[r0:r0 + GLA_CHUNK]
        kc = k[r0:r0 + GLA_CHUNK]
        qc = q[r0:r0 + GLA_CHUNK]
        vc = v[r0:r0 + GLA_CHUNK]
        b_last = bc[GLA_CHUNK - 1:GLA_CHUNK, :]
        k_dec = (kc * jnp.exp(b_last - bc)).astype(BF16)
        q_in = (qc * jnp.exp(bc)).astype(BF16)
        decay = jnp.exp(b_last)
        k_cat = jnp.concatenate(
            [kc[0:i * SUB_BLOCK] * jnp.exp(bc[i * SUB_BLOCK:i * SUB_BLOCK + 1] - bc[0:i * SUB_BLOCK])
             for i in range(1, N_SUB)], axis=0).astype(BF16)
        v_cat = jnp.concatenate([vc[0:i * SUB_BLOCK] for i in range(1, N_SUB)], axis=0)
        a_dg = jnp.where(diag_mask, a_diag[r0:r0 + GLA_CHUNK], 0.0).astype(BF16)
        q_of = q_off[r0:r0 + GLA_CHUNK]
        state = state_ref[...]
        state_b = state.astype(BF16)
        for h in range(GLA_HEADS):
            ks = slice(h * GLA_DK, (h + 1) * GLA_DK)
            vs = slice(h * GLA_DV, (h + 1) * GLA_DV)
            p = _dot(q_of[:, ks], k_cat[:, ks], NT)
            p = jnp.where(off_mask, p, 0.0).astype(BF16)
            o_h = (_dot(p, v_cat[:, vs]) + _dot(a_dg[:, ks], vc[:, vs])
                   + _dot(q_in[:, ks], state_b[:, ks], NT))
            o_ref[r0:r0 + GLA_CHUNK, vs] = o_h
            state_ref[:, ks] = state[:, ks] * decay[:, ks] + _dot(vc[:, vs], k_dec[:, ks], TN)

    g_out = vgu_ref[:, GLA_VAL:2 * GLA_VAL]
    gated = []
    for h in range(GLA_HEADS):
        vs = slice(h * GLA_DV, (h + 1) * GLA_DV)
        o_h = _rmsnorm(o_ref[:, vs], gnorm_ref[...])
        g_h = g_out[:, vs]
        gated.append((o_h * (g_h * jax.nn.sigmoid(g_h))).astype(BF16))
    mix = _dot(jnp.concatenate(gated, axis=1), wout_ref[0:GLA_VAL, :])

    u = vgu_ref[:, 2 * GLA_VAL:2 * GLA_VAL + POOL_WIDTH]
    ubuf_ref[POOL_HALO:POOL_HALO + tm, :] = u
    pos = t * tm + lax.broadcasted_iota(jnp.int32, (tm, 1), 0)
    pooled = []
    for g, w in enumerate(POOL_WINDOWS):
        ls = slice(g * POOL_GC, (g + 1) * POOL_GC)
        acc = u[:, ls]
        for s in range(1, w):
            acc = acc + ubuf_ref[POOL_HALO - s:POOL_HALO - s + tm, ls]
        cnt = jnp.minimum(pos + 1, w).astype(F32)
        y = _dot((acc / cnt - u[:, ls]).astype(BF16), wpool_ref[g])
        pooled.append((y * pscale_ref[:, ls]).astype(BF16))
    ubuf_ref[0:POOL_HALO, :] = ubuf_ref[tm:tm + POOL_HALO, :]
    mix = mix + _dot(jnp.concatenate(pooled, axis=1), wout_ref[GLA_VAL:, :])

    h_ref[...] = x_ref[...] + mix


def _ffn_kernel(h_ref, p_ref, lnf_ref, wg_ref, wu_ref, wd_ref, lnp_ref, wpg_ref, wpp_ref,
                lnfin_ref, out_ref, hid_ref):
    h = h_ref[...]
    f = _rmsnorm(h, lnf_ref[...]).astype(BF16)
    for c in range(0, D_FF, FF_CHUNK):
        g = _dot(f, wg_ref[:, c:c + FF_CHUNK])
        u = _dot(f, wu_ref[:, c:c + FF_CHUNK])
        hid_ref[:, c:c + FF_CHUNK] = (g * jax.nn.sigmoid(g) * u).astype(BF16)
    h = h + _dot(hid_ref[...], wd_ref[...])
    e = _rmsnorm(h, lnp_ref[...]).astype(BF16)
    gate = jax.nn.sigmoid(_dot(e, wpg_ref[...]))
    h = h + gate * _dot(p_ref[...].astype(BF16), wpp_ref[...])
    out_ref[...] = _rmsnorm(h, lnfin_ref[...])


def _selection_matrix():
    r = jnp.arange(SUB_BLOCK * GLA_KEY)
    j, hd = r // GLA_KEY, r % GLA_KEY
    c = jnp.arange(GLA_KEY)
    same_head = (hd // GLA_DK)[:, None] == (c // GLA_CHUNK)[None, :]
    same_col = j[:, None] == (c % SUB_BLOCK)[None, :]
    return (same_head & same_col).astype(BF16)


def _chunk_tril(tm):
    r = jnp.arange(tm)
    same_chunk = (r // GLA_CHUNK)[:, None] == (r // GLA_CHUNK)[None, :]
    return (same_chunk & (r[None, :] <= r[:, None])).astype(BF16)


@functools.partial(jax.jit, static_argnames=("tm",))
def _forward(x, p, ln_mix, w_in, w_gate_lr, b_gate, gla_norm, w_pool, pool_scale, w_out,
             ln_ffn, w_ffn_gate, w_ffn_up, w_ffn_down, ln_ple, w_ple_gate, w_ple_proj,
             ln_final, tm=512):
    bsz, seq, _ = x.shape
    n_tok = bsz * seq
    h = x.reshape(n_tok, D_MODEL)
    assert ln_mix.shape[0] == 1
    for i in range(1):
        s_glr = 2 * GLA_KEY + 2 * GLA_VAL
        w_main = jnp.concatenate(
            [w_in[i][:, :s_glr], w_in[i][:, s_glr + GLA_GATE_RANK:]], axis=1).astype(BF16)
        w_glr = jnp.pad(w_in[i][:, s_glr:s_glr + GLA_GATE_RANK],
                        ((0, 0), (0, LANES - GLA_GATE_RANK))).astype(BF16)
        w_gate = jnp.pad(w_gate_lr[i], ((0, LANES - GLA_GATE_RANK), (0, 0))).astype(BF16)

        qkg, vgu = pl.pallas_call(
            _inproj_kernel,
            grid=(n_tok // tm,),
            in_specs=[
                pl.BlockSpec((tm, D_MODEL), lambda r: (r, 0)),
                _const_spec((1, D_MODEL)),
                _const_spec((D_MODEL, 2 * GLA_KEY + 2 * GLA_VAL + POOL_WIDTH)),
                _const_spec((D_MODEL, LANES)),
                _const_spec((LANES, GLA_KEY)),
                _const_spec((1, GLA_KEY)),
            ],
            out_specs=[
                pl.BlockSpec((tm, 3 * GLA_KEY), lambda r: (r, 0)),
                pl.BlockSpec((tm, 2 * GLA_VAL + POOL_WIDTH), lambda r: (r, 0)),
            ],
            out_shape=[
                jax.ShapeDtypeStruct((n_tok, 3 * GLA_KEY), F32),
                jax.ShapeDtypeStruct((n_tok, 2 * GLA_VAL + POOL_WIDTH), F32),
            ],
            compiler_params=pltpu.CompilerParams(
                dimension_semantics=("arbitrary",), vmem_limit_bytes=VMEM_LIMIT),
            name="inproj",
        )(h, ln_mix[i][None], w_main, w_glr, w_gate, b_gate[i][None])

        tiles = seq // tm
        h = pl.pallas_call(
            functools.partial(_mix_kernel, tm=tm),
            grid=(bsz, tiles),
            in_specs=[
                pl.BlockSpec((tm, 3 * GLA_KEY), lambda b, t: (b * tiles + t, 0)),
                pl.BlockSpec((tm, 2 * GLA_VAL + POOL_WIDTH), lambda b, t: (b * tiles + t, 0)),
                pl.BlockSpec((tm, D_MODEL), lambda b, t: (b * tiles + t, 0)),
                _const_spec((tm, tm)),
                _const_spec((SUB_BLOCK * GLA_KEY, GLA_KEY)),
                _const_spec((1, GLA_DV)),
                _const_spec((len(POOL_WINDOWS), POOL_GC, POOL_GC)),
                _const_spec((1, POOL_WIDTH)),
                _const_spec((D_MODEL, D_MODEL)),
            ],
            out_specs=pl.BlockSpec((tm, D_MODEL), lambda b, t: (b * tiles + t, 0)),
            out_shape=jax.ShapeDtypeStruct((n_tok, D_MODEL), F32),
            scratch_shapes=[
                pltpu.VMEM((GLA_DV, GLA_KEY), F32),
                pltpu.VMEM((POOL_HALO + tm, POOL_WIDTH), F32),
                pltpu.VMEM((tm, SUB_BLOCK * GLA_KEY), BF16),
                pltpu.VMEM((tm, GLA_VAL), F32),
            ],
            compiler_params=pltpu.CompilerParams(
                dimension_semantics=("arbitrary", "arbitrary"), vmem_limit_bytes=VMEM_LIMIT),
            name="mix",
        )(qkg, vgu, h, _chunk_tril(tm), _selection_matrix(), gla_norm[i][None],
          w_pool[i].astype(BF16), pool_scale[i][None], w_out[i].astype(BF16))

        h = pl.pallas_call(
            _ffn_kernel,
            grid=(n_tok // tm,),
            in_specs=[
                pl.BlockSpec((tm, D_MODEL), lambda r: (r, 0)),
                pl.BlockSpec((tm, D_PLE), lambda r: (r, 0)),
                _const_spec((1, D_MODEL)),
                _const_spec((D_MODEL, D_FF)),
                _const_spec((D_MODEL, D_FF)),
                _const_spec((D_FF, D_MODEL)),
                _const_spec((1, D_MODEL)),
                _const_spec((D_MODEL, D_MODEL)),
                _const_spec((D_PLE, D_MODEL)),
                _const_spec((1, D_MODEL)),
            ],
            out_specs=pl.BlockSpec((tm, D_MODEL), lambda r: (r, 0)),
            out_shape=jax.ShapeDtypeStruct((n_tok, D_MODEL), F32),
            scratch_shapes=[pltpu.VMEM((tm, D_FF), BF16)],
            compiler_params=pltpu.CompilerParams(
                dimension_semantics=("arbitrary",), vmem_limit_bytes=VMEM_LIMIT),
            name="ffn",
        )(h, p[i].reshape(n_tok, D_PLE), ln_ffn[i][None], w_ffn_gate[i].astype(BF16),
          w_ffn_up[i].astype(BF16), w_ffn_down[i].astype(BF16), ln_ple[i][None],
          w_ple_gate[i].astype(BF16), w_ple_proj[i].astype(BF16),
          ln_final[None])
    return h.reshape(bsz, seq, D_MODEL)


def kernel(x, p, ln_mix, w_in, w_gate_lr, b_gate, gla_norm, w_pool, pool_scale, w_out, ln_ffn,
           w_ffn_gate, w_ffn_up, w_ffn_down, ln_ple, w_ple_gate, w_ple_proj, ln_final):
    return _forward(x, p, ln_mix, w_in, w_gate_lr, b_gate, gla_norm, w_pool, pool_scale, w_out,
                    ln_ffn, w_ffn_gate, w_ffn_up, w_ffn_down, ln_ple, w_ple_gate, w_ple_proj,
                    ln_final)
```

```python
import functools
import math

import jax
import jax.numpy as jnp
from jax import lax
from jax.experimental import pallas as pl
from jax.experimental.pallas import tpu as pltpu

D_MODEL = 1024
D_PLE = 256
GLA_HEADS = 4
GLA_DK = 64
GLA_DV = 128
GLA_KEY = GLA_HEADS * GLA_DK
GLA_VAL = GLA_HEADS * GLA_DV
GLA_GATE_RANK = 16
GLA_GATE_NORM = 16.0
GLA_CHUNK = 64
POOL_WINDOWS = (2, 4, 8, 16)
POOL_GC = 128
POOL_WIDTH = 512
D_FF = 2816
EPS = 1e-6

LANES = 128
SUBLANES = 8
KEY_TILES = GLA_KEY // LANES
HEADS_PER_TILE = LANES // GLA_DK
SUB_BLOCK = 16
N_SUB = GLA_CHUNK // SUB_BLOCK
N_CAT = SUB_BLOCK * (N_SUB * (N_SUB - 1) // 2)
TRIL_ROWS = 128
DIAG_ROWS = 256
POOL_HALO = max(POOL_WINDOWS)
FF_CHUNK = 256
TM_IN = 1024
TM_MIX = 512
TM_FFN = 1024
VMEM_LIMIT = 56 * 1024 * 1024
LOG2E = math.log2(math.e)

F32 = jnp.float32
BF16 = jnp.bfloat16
NT = (((1,), (1,)), ((), ()))
TN = (((0,), (0,)), ((), ()))


def _dot(a, b, dims=None):
    if dims is None:
        return jnp.dot(a, b, preferred_element_type=F32)
    return lax.dot_general(a, b, dims, preferred_element_type=F32)


def _rmsnorm(x, g):
    return x * lax.rsqrt(jnp.mean(x * x, axis=-1, keepdims=True) + EPS) * g


def _const_spec(shape):
    nd = len(shape)
    return pl.BlockSpec(shape, lambda *_: (0,) * nd, pipeline_mode=pl.Buffered(1))


def _inproj_kernel(x_ref, ln_ref, wmain_ref, wglr_ref, wgate_ref, bgate_ref,
                   qkg_ref, vg_ref, u_ref):
    a = _rmsnorm(x_ref[...], ln_ref[...]).astype(BF16)
    z = _dot(a, wmain_ref[...])
    g_lr = _dot(a, wglr_ref[...])
    gate = _dot(g_lr.astype(BF16), wgate_ref[...]) + bgate_ref[...]
    gk = (jnp.minimum(gate, 0.0) - jnp.log1p(jnp.exp(-jnp.abs(gate)))) * (LOG2E / GLA_GATE_NORM)
    for i in range(KEY_TILES):
        ls = slice(i * LANES, (i + 1) * LANES)
        qkg_ref[i] = z[:, ls] * (GLA_DK ** -0.5)
        qkg_ref[KEY_TILES + i] = z[:, GLA_KEY + i * LANES:GLA_KEY + (i + 1) * LANES]
        qkg_ref[2 * KEY_TILES + i] = gk[:, ls]
    vg_ref[...] = z[:, 2 * GLA_KEY:2 * GLA_KEY + 2 * GLA_VAL].astype(BF16)
    u_ref[...] = z[:, 2 * GLA_KEY + 2 * GLA_VAL:]


def _mix_kernel(qkg_ref, vg_ref, u_ref, x_ref, tril_ref, esel_ref, gnorm_ref, wpool_ref,
                pscale_ref, wout_ref, h_ref, state_ref, ubuf_ref, b_ref, w_ref, qoff_ref,
                o_ref, *, tm):
    t = pl.program_id(1)
    n_chunks = tm // GLA_CHUNK

    @pl.when(t == 0)
    def _():
        state_ref[...] = jnp.zeros_like(state_ref)
        ubuf_ref[0:POOL_HALO, :] = jnp.zeros((POOL_HALO, POOL_WIDTH), F32)

    def q_tile(i):
        return qkg_ref.at[i]

    def k_tile(i):
        return qkg_ref.at[KEY_TILES + i]

    gk = jnp.concatenate([qkg_ref[2 * KEY_TILES + i] for i in range(KEY_TILES)], axis=1)
    g_hi = gk.astype(BF16)
    rem = gk - g_hi.astype(F32)
    g_mid = rem.astype(BF16)
    g_lo = (rem - g_mid.astype(F32)).astype(BF16)
    tril = tril_ref[...]
    for r0 in range(0, tm, TRIL_ROWS):
        rs = slice(r0, r0 + TRIL_ROWS)
        b_blk = _dot(tril, g_hi[rs]) + _dot(tril, g_mid[rs]) + _dot(tril, g_lo[rs])
        for i in range(KEY_TILES):
            b_ref[i, rs, :] = b_blk[:, i * LANES:(i + 1) * LANES]

    row8 = lax.broadcasted_iota(jnp.int32, (SUBLANES, LANES), 0)
    neg_inf = jnp.full((SUBLANES, LANES), -jnp.inf, F32)
    zeros8 = jnp.zeros((SUBLANES, LANES), F32)

    def group_body(g):
        r0 = g * SUB_BLOCK
        for i in range(KEY_TILES):
            b_lo = b_ref[i, pl.ds(r0, SUBLANES), :]
            b_hi = b_ref[i, pl.ds(r0 + SUBLANES, SUBLANES), :]
            q_lo = q_tile(i)[pl.ds(r0, SUBLANES), :]
            q_hi = q_tile(i)[pl.ds(r0 + SUBLANES, SUBLANES), :]
            for j in range(SUB_BLOCK):
                b_j = b_ref[i, pl.ds(r0 + j, SUBLANES, stride=0), :]
                k_j = k_tile(i)[pl.ds(r0 + j, SUBLANES, stride=0), :]
                if j < SUBLANES:
                    e_lo = jnp.exp2(jnp.where(row8 >= j, b_lo - b_j, neg_inf))
                    e_hi = jnp.exp2(b_hi - b_j)
                    w_lo = q_lo * k_j * e_lo
                else:
                    e_hi = jnp.exp2(jnp.where(row8 >= j - SUBLANES, b_hi - b_j, neg_inf))
                    w_lo = zeros8
                w_hi = q_hi * k_j * e_hi
                col = j * GLA_KEY + i * LANES
                w_ref[pl.ds(r0, SUB_BLOCK), col:col + LANES] = (
                    jnp.concatenate([w_lo, w_hi], axis=0).astype(BF16))
                if j == 0:
                    qoff_ref[pl.ds(r0, SUB_BLOCK), i * LANES:(i + 1) * LANES] = (
                        jnp.concatenate([q_lo * e_lo, q_hi * e_hi], axis=0))
    a_parts = []
    for r0 in range(0, tm, DIAG_ROWS):
        for g in range(r0 // SUB_BLOCK, (r0 + DIAG_ROWS) // SUB_BLOCK):
            group_body(g)
        a_parts.append(_dot(w_ref[r0:r0 + DIAG_ROWS, :], esel_ref[...]))
    a_diag = jnp.concatenate(a_parts, axis=0)

    lane_head = (lax.broadcasted_iota(jnp.int32, (tm, GLA_KEY), 1) % LANES) // GLA_DK
    blk_row = (lax.broadcasted_iota(jnp.int32, (tm, GLA_KEY), 0) % GLA_CHUNK) // SUB_BLOCK
    blk_lane = (lax.broadcasted_iota(jnp.int32, (tm, GLA_KEY), 1) % GLA_CHUNK) // SUB_BLOCK
    b_all = jnp.concatenate([b_ref[i] for i in range(KEY_TILES)], axis=1)
    q_all = jnp.concatenate([qkg_ref[i] for i in range(KEY_TILES)], axis=1)
    q_in_all = q_all * jnp.exp2(b_all)
    q_off_all = qoff_ref[...]
    a_dg, q_in, q_off = [], [], []
    for hh in range(HEADS_PER_TILE):
        sel = lane_head == hh
        a_dg.append(jnp.where(sel & (blk_row == blk_lane), a_diag, 0.0).astype(BF16))
        q_in.append(jnp.where(sel, q_in_all, 0.0).astype(BF16))
        q_off.append(jnp.where(sel, q_off_all, 0.0).astype(BF16))

    prow = lax.broadcasted_iota(jnp.int32, (GLA_CHUNK, N_CAT), 0) // SUB_BLOCK
    pcol = lax.broadcasted_iota(jnp.int32, (GLA_CHUNK, N_CAT), 1)
    pgrp = jnp.zeros_like(pcol)
    start = 0
    for i in range(1, N_SUB):
        pgrp = jnp.where((pcol >= start) & (pcol < start + i * SUB_BLOCK), i, pgrp)
        start += i * SUB_BLOCK
    off_mask = prow == pgrp
    first_head = lax.broadcasted_iota(jnp.int32, (GLA_DV, LANES), 1) < GLA_DK

    p_off = {}
    d_state = {}
    decay = {}
    for c in range(n_chunks):
        r0 = c * GLA_CHUNK
        rs = slice(r0, r0 + GLA_CHUNK)
        for i in range(KEY_TILES):
            bc = b_ref[i, rs, :]
            kc = k_tile(i)[rs, :]
            b_last = b_ref[i, pl.ds(r0 + GLA_CHUNK - 1, GLA_CHUNK, stride=0), :]
            k_dec = (kc * jnp.exp2(b_last - bc)).astype(BF16)
            decay[c, i] = jnp.exp2(b_last[0:SUBLANES])
            pieces = []
            for s in range(1, N_SUB):
                n = s * SUB_BLOCK
                r_s = b_ref[i, pl.ds(r0 + n, n, stride=0), :]
                pieces.append(kc[0:n] * jnp.exp2(r_s - bc[0:n]))
            k_cat = jnp.concatenate(pieces, axis=0).astype(BF16)
            halves = []
            for hh in range(HEADS_PER_TILE):
                h = i * HEADS_PER_TILE + hh
                vs = slice(h * GLA_DV, (h + 1) * GLA_DV)
                p = _dot(q_off[hh][rs, i * LANES:(i + 1) * LANES], k_cat, NT)
                p_off[c, h] = jnp.where(off_mask, p, 0.0).astype(BF16)
                halves.append(_dot(vg_ref[rs, vs], k_dec, TN))
            d_state[c, i] = jnp.where(first_head, halves[0], halves[1])

    state_in = {}
    for i in range(KEY_TILES):
        ls = slice(i * LANES, (i + 1) * LANES)
        s = state_ref[:, ls]
        for c in range(n_chunks):
            state_in[c, i] = s.astype(BF16)
            s = (s.reshape(GLA_DV // SUBLANES, SUBLANES, LANES) * decay[c, i][None]
                 ).reshape(GLA_DV, LANES) + d_state[c, i]
        state_ref[:, ls] = s

    for c in range(n_chunks):
        r0 = c * GLA_CHUNK
        rs = slice(r0, r0 + GLA_CHUNK)
        for h in range(GLA_HEADS):
            i, hh = divmod(h, HEADS_PER_TILE)
            ls = slice(i * LANES, (i + 1) * LANES)
            vs = slice(h * GLA_DV, (h + 1) * GLA_DV)
            vc = vg_ref[rs, vs]
            v_cat = jnp.concatenate([vc[0:s * SUB_BLOCK] for s in range(1, N_SUB)], axis=0)
            v_two = jnp.concatenate([vc, vc], axis=0)
            o_ref[rs, vs] = (_dot(p_off[c, h], v_cat) + _dot(a_dg[hh][rs, ls], v_two)
                             + _dot(q_in[hh][rs, ls], state_in[c, i], NT))

    gated = []
    for h in range(GLA_HEADS):
        vs = slice(h * GLA_DV, (h + 1) * GLA_DV)
        o_h = _rmsnorm(o_ref[:, vs], gnorm_ref[...])
        g_h = vg_ref[:, GLA_VAL + h * GLA_DV:GLA_VAL + (h + 1) * GLA_DV].astype(F32)
        gated.append((o_h * (g_h * jax.nn.sigmoid(g_h))).astype(BF16))
    mix = _dot(jnp.concatenate(gated, axis=1), wout_ref[0:GLA_VAL, :])

    u = u_ref[...]
    ubuf_ref[POOL_HALO:POOL_HALO + tm, :] = u
    pos = t * tm + lax.broadcasted_iota(jnp.int32, (tm, 1), 0)
    pooled = []
    for g, w in enumerate(POOL_WINDOWS):
        ls = slice(g * POOL_GC, (g + 1) * POOL_GC)
        acc = ubuf_ref[:, ls]
        lo, step = 0, 1
        while step < w:
            acc = acc[step:] + acc[:-step]
            lo += step
            step *= 2
        acc = acc[POOL_HALO - lo:]
        cnt = jnp.minimum(pos + 1, w).astype(F32)
        y = _dot((acc / cnt - u[:, ls]).astype(BF16), wpool_ref[g])
        pooled.append((y * pscale_ref[:, ls]).astype(BF16))
    ubuf_ref[0:POOL_HALO, :] = ubuf_ref[tm:tm + POOL_HALO, :]
    mix = mix + _dot(jnp.concatenate(pooled, axis=1), wout_ref[GLA_VAL:, :])

    h_ref[...] = x_ref[...] + mix


def _ffn_kernel(h_ref, p_ref, lnf_ref, wg_ref, wu_ref, wd_ref, lnp_ref, wpg_ref, wpp_ref,
                lnfin_ref, out_ref, hid_ref):
    h = h_ref[...]
    f = _rmsnorm(h, lnf_ref[...]).astype(BF16)
    for c in range(0, D_FF, FF_CHUNK):
        g = _dot(f, wg_ref[:, c:c + FF_CHUNK])
        u = _dot(f, wu_ref[:, c:c + FF_CHUNK])
        hid_ref[:, c:c + FF_CHUNK] = (g * jax.nn.sigmoid(g) * u).astype(BF16)
    h = h + _dot(hid_ref[...], wd_ref[...])
    e = _rmsnorm(h, lnp_ref[...]).astype(BF16)
    gate = jax.nn.sigmoid(_dot(e, wpg_ref[...]))
    h = h + gate * _dot(p_ref[...].astype(BF16), wpp_ref[...])
    out_ref[...] = _rmsnorm(h, lnfin_ref[...])


def _selection_matrix():
    r = jnp.arange(SUB_BLOCK * GLA_KEY)
    j, hd = r // GLA_KEY, r % GLA_KEY
    c = jnp.arange(GLA_KEY)
    same_head = (hd // GLA_DK)[:, None] == (c // GLA_CHUNK)[None, :]
    same_col = j[:, None] == (c % SUB_BLOCK)[None, :]
    return (same_head & same_col).astype(BF16)


def _chunk_tril():
    r = jnp.arange(TRIL_ROWS)
    same_chunk = (r // GLA_CHUNK)[:, None] == (r // GLA_CHUNK)[None, :]
    return (same_chunk & (r[None, :] <= r[:, None])).astype(BF16)


@jax.jit
def _forward(x, p, ln_mix, w_in, w_gate_lr, b_gate, gla_norm, w_pool, pool_scale, w_out,
             ln_ffn, w_ffn_gate, w_ffn_up, w_ffn_down, ln_ple, w_ple_gate, w_ple_proj,
             ln_final):
    bsz, seq, _ = x.shape
    n_tok = bsz * seq
    h = x.reshape(n_tok, D_MODEL)
    assert ln_mix.shape[0] == 1
    s_glr = 2 * GLA_KEY + 2 * GLA_VAL
    w_main = jnp.concatenate(
        [w_in[0][:, :s_glr], w_in[0][:, s_glr + GLA_GATE_RANK:]], axis=1).astype(BF16)
    w_glr = jnp.pad(w_in[0][:, s_glr:s_glr + GLA_GATE_RANK],
                    ((0, 0), (0, LANES - GLA_GATE_RANK))).astype(BF16)
    w_gate = jnp.pad(w_gate_lr[0], ((0, LANES - GLA_GATE_RANK), (0, 0))).astype(BF16)

    qkg, vg, u = pl.pallas_call(
        _inproj_kernel,
        grid=(n_tok // TM_IN,),
        in_specs=[
            pl.BlockSpec((TM_IN, D_MODEL), lambda r: (r, 0)),
            _const_spec((1, D_MODEL)),
            _const_spec((D_MODEL, 2 * GLA_KEY + 2 * GLA_VAL + POOL_WIDTH)),
            _const_spec((D_MODEL, LANES)),
            _const_spec((LANES, GLA_KEY)),
            _const_spec((1, GLA_KEY)),
        ],
        out_specs=[
            pl.BlockSpec((3 * KEY_TILES, TM_IN, LANES), lambda r: (0, r, 0)),
            pl.BlockSpec((TM_IN, 2 * GLA_VAL), lambda r: (r, 0)),
            pl.BlockSpec((TM_IN, POOL_WIDTH), lambda r: (r, 0)),
        ],
        out_shape=[
            jax.ShapeDtypeStruct((3 * KEY_TILES, n_tok, LANES), F32),
            jax.ShapeDtypeStruct((n_tok, 2 * GLA_VAL), BF16),
            jax.ShapeDtypeStruct((n_tok, POOL_WIDTH), F32),
        ],
        compiler_params=pltpu.CompilerParams(
            dimension_semantics=("arbitrary",), vmem_limit_bytes=VMEM_LIMIT),
        name="inproj",
    )(h, ln_mix[0][None], w_main, w_glr, w_gate, b_gate[0][None])

    tm = TM_MIX
    tiles = seq // tm
    h = pl.pallas_call(
        functools.partial(_mix_kernel, tm=tm),
        grid=(bsz, tiles),
        in_specs=[
            pl.BlockSpec((3 * KEY_TILES, tm, LANES), lambda b, t: (0, b * tiles + t, 0)),
            pl.BlockSpec((tm, 2 * GLA_VAL), lambda b, t: (b * tiles + t, 0)),
            pl.BlockSpec((tm, POOL_WIDTH), lambda b, t: (b * tiles + t, 0)),
            pl.BlockSpec((tm, D_MODEL), lambda b, t: (b * tiles + t, 0)),
            _const_spec((TRIL_ROWS, TRIL_ROWS)),
            _const_spec((SUB_BLOCK * GLA_KEY, GLA_KEY)),
            _const_spec((1, GLA_DV)),
            _const_spec((len(POOL_WINDOWS), POOL_GC, POOL_GC)),
            _const_spec((1, POOL_WIDTH)),
            _const_spec((D_MODEL, D_MODEL)),
        ],
        out_specs=pl.BlockSpec((tm, D_MODEL), lambda b, t: (b * tiles + t, 0)),
        out_shape=jax.ShapeDtypeStruct((n_tok, D_MODEL), F32),
        scratch_shapes=[
            pltpu.VMEM((GLA_DV, GLA_KEY), F32),
            pltpu.VMEM((POOL_HALO + tm, POOL_WIDTH), F32),
            pltpu.VMEM((KEY_TILES, tm, LANES), F32),
            pltpu.VMEM((tm, SUB_BLOCK * GLA_KEY), BF16),
            pltpu.VMEM((tm, GLA_KEY), F32),
            pltpu.VMEM((tm, GLA_VAL), F32),
        ],
        compiler_params=pltpu.CompilerParams(
            dimension_semantics=("arbitrary", "arbitrary"), vmem_limit_bytes=VMEM_LIMIT),
        name="mix",
    )(qkg, vg, u, h, _chunk_tril(), _selection_matrix(), gla_norm[0][None],
      w_pool[0].astype(BF16), pool_scale[0][None], w_out[0].astype(BF16))

    h = pl.pallas_call(
        _ffn_kernel,
        grid=(n_tok // TM_FFN,),
        in_specs=[
            pl.BlockSpec((TM_FFN, D_MODEL), lambda r: (r, 0)),
            pl.BlockSpec((TM_FFN, D_PLE), lambda r: (r, 0)),
            _const_spec((1, D_MODEL)),
            _const_spec((D_MODEL, D_FF)),
            _const_spec((D_MODEL, D_FF)),
            _const_spec((D_FF, D_MODEL)),
            _const_spec((1, D_MODEL)),
            _const_spec((D_MODEL, D_MODEL)),
            _const_spec((D_PLE, D_MODEL)),
            _const_spec((1, D_MODEL)),
        ],
        out_specs=pl.BlockSpec((TM_FFN, D_MODEL), lambda r: (r, 0)),
        out_shape=jax.ShapeDtypeStruct((n_tok, D_MODEL), F32),
        scratch_shapes=[pltpu.VMEM((TM_FFN, D_FF), BF16)],
        compiler_params=pltpu.CompilerParams(
            dimension_semantics=("arbitrary",), vmem_limit_bytes=VMEM_LIMIT),
        name="ffn",
    )(h, p[0].reshape(n_tok, D_PLE), ln_ffn[0][None], w_ffn_gate[0].astype(BF16),
      w_ffn_up[0].astype(BF16), w_ffn_down[0].astype(BF16), ln_ple[0][None],
      w_ple_gate[0].astype(BF16), w_ple_proj[0].astype(BF16), ln_final[None])
    return h.reshape(bsz, seq, D_MODEL)


def kernel(x, p, ln_mix, w_in, w_gate_lr, b_gate, gla_norm, w_pool, pool_scale, w_out, ln_ffn,
           w_ffn_gate, w_ffn_up, w_ffn_down, ln_ple, w_ple_gate, w_ple_proj, ln_final):
    return _forward(x, p, ln_mix, w_in, w_gate_lr, b_gate, gla_norm, w_pool, pool_scale, w_out,
                    ln_ffn, w_ffn_gate, w_ffn_up, w_ffn_down, ln_ple, w_ple_gate, w_ple_proj,
                    ln_final)
```

```python
import functools
import math

import jax
import jax.numpy as jnp
from jax import lax
from jax.experimental import pallas as pl
from jax.experimental.pallas import tpu as pltpu

D_MODEL = 1024
D_PLE = 256
GLA_HEADS = 4
GLA_DK = 64
GLA_DV = 128
GLA_KEY = GLA_HEADS * GLA_DK
GLA_VAL = GLA_HEADS * GLA_DV
GLA_GATE_RANK = 16
GLA_GATE_NORM = 16.0
GLA_CHUNK = 64
POOL_WINDOWS = (2, 4, 8, 16)
POOL_GC = 128
POOL_WIDTH = 512
D_FF = 2816
EPS = 1e-6

LANES = 128
SUBLANES = 8
KEY_TILES = GLA_KEY // LANES
HEADS_PER_TILE = LANES // GLA_DK
SUB_BLOCK = 16
N_SUB = GLA_CHUNK // SUB_BLOCK
N_CAT = SUB_BLOCK * (N_SUB * (N_SUB - 1) // 2)
TRIL_ROWS = 128
DIAG_ROWS = 256
PROJ_COLS = 256
PROJECT_AFTER_GROUP = (3, 7, 11, 23)
PROJECT_AFTER_SCORES = (3, 7)
PROJECT_AFTER_OUTPUTS = (3,)
POOL_HALO = max(POOL_WINDOWS)
FF_CHUNK = 256
TM_MIX = 512
TM_FFN = 1024
VMEM_LIMIT = 56 * 1024 * 1024
LOG2E = math.log2(math.e)

F32 = jnp.float32
BF16 = jnp.bfloat16
NT = (((1,), (1,)), ((), ()))
TN = (((0,), (0,)), ((), ()))


def _dot(a, b, dims=None):
    if dims is None:
        return jnp.dot(a, b, preferred_element_type=F32)
    return lax.dot_general(a, b, dims, preferred_element_type=F32)


def _rmsnorm(x, g):
    return x * lax.rsqrt(jnp.mean(x * x, axis=-1, keepdims=True) + EPS) * g


def _const_spec(shape):
    nd = len(shape)
    return pl.BlockSpec(shape, lambda *_: (0,) * nd, pipeline_mode=pl.Buffered(1))


def _projection_steps(x_ref, ln_ref, wmain_ref, wglr_ref, wgate_ref, bgate_ref,
                      qkg_out, vg_out, u_out):
    hold = {}

    def norm():
        hold["a"] = _rmsnorm(x_ref[...], ln_ref[...]).astype(BF16)

    def piece(col):
        def run():
            z = _dot(hold["a"], wmain_ref[:, col:col + PROJ_COLS])
            for i in range(PROJ_COLS // LANES):
                zi = z[:, i * LANES:(i + 1) * LANES]
                c = col + i * LANES
                if c < GLA_KEY:
                    qkg_out[c // LANES] = zi * (GLA_DK ** -0.5)
                elif c < 2 * GLA_KEY:
                    qkg_out[c // LANES] = zi
                elif c < 2 * GLA_KEY + 2 * GLA_VAL:
                    vg_out[:, c - 2 * GLA_KEY:c - 2 * GLA_KEY + LANES] = zi.astype(BF16)
                else:
                    c -= 2 * GLA_KEY + 2 * GLA_VAL
                    u_out[:, c:c + LANES] = zi
        return run

    def gate():
        g_lr = _dot(hold["a"], wglr_ref[...])
        pre = _dot(g_lr.astype(BF16), wgate_ref[...]) + bgate_ref[...]
        gk = (jnp.minimum(pre, 0.0) - jnp.log1p(jnp.exp(-jnp.abs(pre)))) * (LOG2E / GLA_GATE_NORM)
        for i in range(KEY_TILES):
            qkg_out[2 * KEY_TILES + i] = gk[:, i * LANES:(i + 1) * LANES]

    n_cols = 2 * GLA_KEY + 2 * GLA_VAL + POOL_WIDTH
    return [norm] + [piece(c) for c in range(0, n_cols, PROJ_COLS)] + [gate]


def _mix_kernel(x_next_ref, x_ref, ln_ref, wmain_ref, wglr_ref, wgate_ref, bgate_ref,
                tril_ref, esel_ref, gnorm_ref, wpool_ref, pscale_ref, wout_ref, h_ref,
                qkg_ref, vg_ref, u_ref, qkg_next_ref, vg_next_ref, u_next_ref, state_ref,
                ubuf_ref, b_ref, w_ref, qoff_ref, o_ref, *, tm, tiles_per_seq):
    n = pl.program_id(0)
    t = n % tiles_per_seq
    n_chunks = tm // GLA_CHUNK
    weights = (ln_ref, wmain_ref, wglr_ref, wgate_ref, bgate_ref)

    @pl.when(n == 0)
    def _():
        for step in _projection_steps(x_ref, *weights, qkg_ref, vg_ref, u_ref):
            step()

    @pl.when(t == 0)
    def _():
        state_ref[...] = jnp.zeros_like(state_ref)
        ubuf_ref[0:POOL_HALO, :] = jnp.zeros((POOL_HALO, POOL_WIDTH), F32)

    pending = _projection_steps(x_next_ref, *weights, qkg_next_ref, vg_next_ref, u_next_ref)

    def project_next(k):
        for _ in range(min(k, len(pending))):
            pending.pop(0)()

    vg = vg_ref
    project_next(1)

    def q_tile(i):
        return qkg_ref.at[i]

    def k_tile(i):
        return qkg_ref.at[KEY_TILES + i]

    gk = jnp.concatenate([qkg_ref[2 * KEY_TILES + i] for i in range(KEY_TILES)], axis=1)
    g_hi = gk.astype(BF16)
    rem = gk - g_hi.astype(F32)
    g_mid = rem.astype(BF16)
    g_lo = (rem - g_mid.astype(F32)).astype(BF16)
    tril = tril_ref[...]
    for r0 in range(0, tm, TRIL_ROWS):
        rs = slice(r0, r0 + TRIL_ROWS)
        b_blk = _dot(tril, g_hi[rs]) + _dot(tril, g_mid[rs]) + _dot(tril, g_lo[rs])
        for i in range(KEY_TILES):
            b_ref[i, rs, :] = b_blk[:, i * LANES:(i + 1) * LANES]

    row8 = lax.broadcasted_iota(jnp.int32, (SUBLANES, LANES), 0)
    neg_inf = jnp.full((SUBLANES, LANES), -jnp.inf, F32)
    zeros8 = jnp.zeros((SUBLANES, LANES), F32)

    def group_body(g):
        r0 = g * SUB_BLOCK
        for i in range(KEY_TILES):
            b_lo = b_ref[i, pl.ds(r0, SUBLANES), :]
            b_hi = b_ref[i, pl.ds(r0 + SUBLANES, SUBLANES), :]
            q_lo = q_tile(i)[pl.ds(r0, SUBLANES), :]
            q_hi = q_tile(i)[pl.ds(r0 + SUBLANES, SUBLANES), :]
            for j in range(SUB_BLOCK):
                b_j = b_ref[i, pl.ds(r0 + j, SUBLANES, stride=0), :]
                k_j = k_tile(i)[pl.ds(r0 + j, SUBLANES, stride=0), :]
                if j < SUBLANES:
                    e_lo = jnp.exp2(jnp.where(row8 >= j, b_lo - b_j, neg_inf))
                    e_hi = jnp.exp2(b_hi - b_j)
                    w_lo = q_lo * k_j * e_lo
                else:
                    e_hi = jnp.exp2(jnp.where(row8 >= j - SUBLANES, b_hi - b_j, neg_inf))
                    w_lo = zeros8
                w_hi = q_hi * k_j * e_hi
                col = j * GLA_KEY + i * LANES
                w_ref[pl.ds(r0, SUB_BLOCK), col:col + LANES] = (
                    jnp.concatenate([w_lo, w_hi], axis=0).astype(BF16))
                if j == 0:
                    qoff_ref[pl.ds(r0, SUB_BLOCK), i * LANES:(i + 1) * LANES] = (
                        jnp.concatenate([q_lo * e_lo, q_hi * e_hi], axis=0))
    a_parts = []
    for r0 in range(0, tm, DIAG_ROWS):
        for g in range(r0 // SUB_BLOCK, (r0 + DIAG_ROWS) // SUB_BLOCK):
            group_body(g)
            if g in PROJECT_AFTER_GROUP:
                project_next(1)
        a_parts.append(_dot(w_ref[r0:r0 + DIAG_ROWS, :], esel_ref[...]))
    a_diag = jnp.concatenate(a_parts, axis=0)

    lane_head = (lax.broadcasted_iota(jnp.int32, (tm, GLA_KEY), 1) % LANES) // GLA_DK
    blk_row = (lax.broadcasted_iota(jnp.int32, (tm, GLA_KEY), 0) % GLA_CHUNK) // SUB_BLOCK
    blk_lane = (lax.broadcasted_iota(jnp.int32, (tm, GLA_KEY), 1) % GLA_CHUNK) // SUB_BLOCK
    b_all = jnp.concatenate([b_ref[i] for i in range(KEY_TILES)], axis=1)
    q_all = jnp.concatenate([q_tile(i)[...] for i in range(KEY_TILES)], axis=1)
    q_in_all = q_all * jnp.exp2(b_all)
    q_off_all = qoff_ref[...]
    a_dg, q_in, q_off = [], [], []
    for hh in range(HEADS_PER_TILE):
        sel = lane_head == hh
        a_dg.append(jnp.where(sel & (blk_row == blk_lane), a_diag, 0.0).astype(BF16))
        q_in.append(jnp.where(sel, q_in_all, 0.0).astype(BF16))
        q_off.append(jnp.where(sel, q_off_all, 0.0).astype(BF16))

    prow = lax.broadcasted_iota(jnp.int32, (GLA_CHUNK, N_CAT), 0) // SUB_BLOCK
    pcol = lax.broadcasted_iota(jnp.int32, (GLA_CHUNK, N_CAT), 1)
    pgrp = jnp.zeros_like(pcol)
    start = 0
    for i in range(1, N_SUB):
        pgrp = jnp.where((pcol >= start) & (pcol < start + i * SUB_BLOCK), i, pgrp)
        start += i * SUB_BLOCK
    off_mask = prow == pgrp
    first_head = lax.broadcasted_iota(jnp.int32, (GLA_DV, LANES), 1) < GLA_DK

    p_off = {}
    d_state = {}
    decay = {}
    for c in range(n_chunks):
        r0 = c * GLA_CHUNK
        rs = slice(r0, r0 + GLA_CHUNK)
        for i in range(KEY_TILES):
            bc = b_ref[i, rs, :]
            kc = k_tile(i)[rs, :]
            b_last = b_ref[i, pl.ds(r0 + GLA_CHUNK - 1, GLA_CHUNK, stride=0), :]
            k_dec = (kc * jnp.exp2(b_last - bc)).astype(BF16)
            decay[c, i] = jnp.exp2(b_last[0:SUBLANES])
            pieces = []
            for s in range(1, N_SUB):
                m = s * SUB_BLOCK
                r_s = b_ref[i, pl.ds(r0 + m, m, stride=0), :]
                pieces.append(kc[0:m] * jnp.exp2(r_s - bc[0:m]))
            k_cat = jnp.concatenate(pieces, axis=0).astype(BF16)
            halves = []
            for hh in range(HEADS_PER_TILE):
                h = i * HEADS_PER_TILE + hh
                vs = slice(h * GLA_DV, (h + 1) * GLA_DV)
                p = _dot(q_off[hh][rs, i * LANES:(i + 1) * LANES], k_cat, NT)
                p_off[c, h] = jnp.where(off_mask, p, 0.0).astype(BF16)
                halves.append(_dot(vg[rs, vs], k_dec, TN))
            d_state[c, i] = jnp.where(first_head, halves[0], halves[1])
        if c in PROJECT_AFTER_SCORES:
            project_next(1)

    state_in = {}
    for i in range(KEY_TILES):
        ls = slice(i * LANES, (i + 1) * LANES)
        s = state_ref[:, ls]
        for c in range(n_chunks):
            state_in[c, i] = s.astype(BF16)
            s = (s.reshape(GLA_DV // SUBLANES, SUBLANES, LANES) * decay[c, i][None]
                 ).reshape(GLA_DV, LANES) + d_state[c, i]
        state_ref[:, ls] = s

    for c in range(n_chunks):
        r0 = c * GLA_CHUNK
        rs = slice(r0, r0 + GLA_CHUNK)
        for h in range(GLA_HEADS):
            i, hh = divmod(h, HEADS_PER_TILE)
            ls = slice(i * LANES, (i + 1) * LANES)
            vs = slice(h * GLA_DV, (h + 1) * GLA_DV)
            vc = vg[rs, vs]
            v_cat = jnp.concatenate([vc[0:s * SUB_BLOCK] for s in range(1, N_SUB)], axis=0)
            v_two = jnp.concatenate([vc, vc], axis=0)
            o_ref[rs, vs] = (_dot(p_off[c, h], v_cat) + _dot(a_dg[hh][rs, ls], v_two)
                             + _dot(q_in[hh][rs, ls], state_in[c, i], NT))
        if c in PROJECT_AFTER_OUTPUTS:
            project_next(1)

    gated = []
    for h in range(GLA_HEADS):
        vs = slice(h * GLA_DV, (h + 1) * GLA_DV)
        o_h = _rmsnorm(o_ref[:, vs], gnorm_ref[...])
        g_h = vg[:, GLA_VAL + h * GLA_DV:GLA_VAL + (h + 1) * GLA_DV].astype(F32)
        gated.append((o_h * (g_h * jax.nn.sigmoid(g_h))).astype(BF16))
    project_next(len(pending))
    mix = _dot(jnp.concatenate(gated, axis=1), wout_ref[0:GLA_VAL, :])

    u = u_ref[...]
    ubuf_ref[POOL_HALO:POOL_HALO + tm, :] = u
    pos = t * tm + lax.broadcasted_iota(jnp.int32, (tm, 1), 0)
    pooled = []
    for g, w in enumerate(POOL_WINDOWS):
        ls = slice(g * POOL_GC, (g + 1) * POOL_GC)
        acc = ubuf_ref[:, ls]
        lo, step = 0, 1
        while step < w:
            acc = acc[step:] + acc[:-step]
            lo += step
            step *= 2
        acc = acc[POOL_HALO - lo:]
        cnt = jnp.minimum(pos + 1, w).astype(F32)
        y = _dot((acc / cnt - u[:, ls]).astype(BF16), wpool_ref[g])
        pooled.append((y * pscale_ref[:, ls]).astype(BF16))
    ubuf_ref[0:POOL_HALO, :] = ubuf_ref[tm:tm + POOL_HALO, :]
    mix = mix + _dot(jnp.concatenate(pooled, axis=1), wout_ref[GLA_VAL:, :])

    h_ref[...] = x_ref[...] + mix

    qkg_ref[...] = qkg_next_ref[...]
    vg_ref[...] = vg_next_ref[...]
    u_ref[...] = u_next_ref[...]


def _ffn_kernel(h_ref, p_ref, lnf_ref, wg_ref, wu_ref, wd_ref, lnp_ref, wpg_ref, wpp_ref,
                lnfin_ref, out_ref, hid_ref):
    h = h_ref[...]
    f = _rmsnorm(h, lnf_ref[...]).astype(BF16)
    for c in range(0, D_FF, FF_CHUNK):
        g = _dot(f, wg_ref[:, c:c + FF_CHUNK])
        u = _dot(f, wu_ref[:, c:c + FF_CHUNK])
        hid_ref[:, c:c + FF_CHUNK] = (g * jax.nn.sigmoid(g) * u).astype(BF16)
    h = h + _dot(hid_ref[...], wd_ref[...])
    e = _rmsnorm(h, lnp_ref[...]).astype(BF16)
    gate = jax.nn.sigmoid(_dot(e, wpg_ref[...]))
    h = h + gate * _dot(p_ref[...].astype(BF16), wpp_ref[...])
    out_ref[...] = _rmsnorm(h, lnfin_ref[...])


def _selection_matrix():
    r = jnp.arange(SUB_BLOCK * GLA_KEY)
    j, hd = r // GLA_KEY, r % GLA_KEY
    c = jnp.arange(GLA_KEY)
    same_head = (hd // GLA_DK)[:, None] == (c // GLA_CHUNK)[None, :]
    same_col = j[:, None] == (c % SUB_BLOCK)[None, :]
    return (same_head & same_col).astype(BF16)


def _chunk_tril():
    r = jnp.arange(TRIL_ROWS)
    same_chunk = (r // GLA_CHUNK)[:, None] == (r // GLA_CHUNK)[None, :]
    return (same_chunk & (r[None, :] <= r[:, None])).astype(BF16)


@jax.jit
def _forward(x, p, ln_mix, w_in, w_gate_lr, b_gate, gla_norm, w_pool, pool_scale, w_out,
             ln_ffn, w_ffn_gate, w_ffn_up, w_ffn_down, ln_ple, w_ple_gate, w_ple_proj,
             ln_final):
    bsz, seq, _ = x.shape
    n_tok = bsz * seq
    h = x.reshape(n_tok, D_MODEL)
    assert ln_mix.shape[0] == 1
    s_glr = 2 * GLA_KEY + 2 * GLA_VAL
    w_main = jnp.concatenate(
        [w_in[0][:, :s_glr], w_in[0][:, s_glr + GLA_GATE_RANK:]], axis=1).astype(BF16)
    w_glr = jnp.pad(w_in[0][:, s_glr:s_glr + GLA_GATE_RANK],
                    ((0, 0), (0, LANES - GLA_GATE_RANK))).astype(BF16)
    w_gate = jnp.pad(w_gate_lr[0], ((0, LANES - GLA_GATE_RANK), (0, 0))).astype(BF16)

    tm = TM_MIX
    n_tiles = n_tok // tm
    h = pl.pallas_call(
        functools.partial(_mix_kernel, tm=tm, tiles_per_seq=seq // tm),
        grid=(n_tiles,),
        in_specs=[
            pl.BlockSpec((tm, D_MODEL), lambda n: (jnp.minimum(n + 1, n_tiles - 1), 0)),
            pl.BlockSpec((tm, D_MODEL), lambda n: (n, 0)),
            _const_spec((1, D_MODEL)),
            _const_spec((D_MODEL, 2 * GLA_KEY + 2 * GLA_VAL + POOL_WIDTH)),
            _const_spec((D_MODEL, LANES)),
            _const_spec((LANES, GLA_KEY)),
            _const_spec((1, GLA_KEY)),
            _const_spec((TRIL_ROWS, TRIL_ROWS)),
            _const_spec((SUB_BLOCK * GLA_KEY, GLA_KEY)),
            _const_spec((1, GLA_DV)),
            _const_spec((len(POOL_WINDOWS), POOL_GC, POOL_GC)),
            _const_spec((1, POOL_WIDTH)),
            _const_spec((D_MODEL, D_MODEL)),
        ],
        out_specs=pl.BlockSpec((tm, D_MODEL), lambda n: (n, 0)),
        out_shape=jax.ShapeDtypeStruct((n_tok, D_MODEL), F32),
        scratch_shapes=[
            pltpu.VMEM((3 * KEY_TILES, tm, LANES), F32),
            pltpu.VMEM((tm, 2 * GLA_VAL), BF16),
            pltpu.VMEM((tm, POOL_WIDTH), F32),
            pltpu.VMEM((3 * KEY_TILES, tm, LANES), F32),
            pltpu.VMEM((tm, 2 * GLA_VAL), BF16),
            pltpu.VMEM((tm, POOL_WIDTH), F32),
            pltpu.VMEM((GLA_DV, GLA_KEY), F32),
            pltpu.VMEM((POOL_HALO + tm, POOL_WIDTH), F32),
            pltpu.VMEM((KEY_TILES, tm, LANES), F32),
            pltpu.VMEM((tm, SUB_BLOCK * GLA_KEY), BF16),
            pltpu.VMEM((tm, GLA_KEY), F32),
            pltpu.VMEM((tm, GLA_VAL), F32),
        ],
        compiler_params=pltpu.CompilerParams(
            dimension_semantics=("arbitrary",), vmem_limit_bytes=VMEM_LIMIT),
        name="mix",
    )(h, h, ln_mix[0][None], w_main, w_glr, w_gate, b_gate[0][None], _chunk_tril(),
      _selection_matrix(), gla_norm[0][None], w_pool[0].astype(BF16), pool_scale[0][None],
      w_out[0].astype(BF16))

    h = pl.pallas_call(
        _ffn_kernel,
        grid=(n_tok // TM_FFN,),
        in_specs=[
            pl.BlockSpec((TM_FFN, D_MODEL), lambda r: (r, 0)),
            pl.BlockSpec((TM_FFN, D_PLE), lambda r: (r, 0)),
            _const_spec((1, D_MODEL)),
            _const_spec((D_MODEL, D_FF)),
            _const_spec((D_MODEL, D_FF)),
            _const_spec((D_FF, D_MODEL)),
            _const_spec((1, D_MODEL)),
            _const_spec((D_MODEL, D_MODEL)),
            _const_spec((D_PLE, D_MODEL)),
            _const_spec((1, D_MODEL)),
        ],
        out_specs=pl.BlockSpec((TM_FFN, D_MODEL), lambda r: (r, 0)),
        out_shape=jax.ShapeDtypeStruct((n_tok, D_MODEL), F32),
        scratch_shapes=[pltpu.VMEM((TM_FFN, D_FF), BF16)],
        compiler_params=pltpu.CompilerParams(
            dimension_semantics=("arbitrary",), vmem_limit_bytes=VMEM_LIMIT),
        name="ffn",
    )(h, p[0].reshape(n_tok, D_PLE), ln_ffn[0][None], w_ffn_gate[0].astype(BF16),
      w_ffn_up[0].astype(BF16), w_ffn_down[0].astype(BF16), ln_ple[0][None],
      w_ple_gate[0].astype(BF16), w_ple_proj[0].astype(BF16), ln_final[None])
    return h.reshape(bsz, seq, D_MODEL)


def kernel(x, p, ln_mix, w_in, w_gate_lr, b_gate, gla_norm, w_pool, pool_scale, w_out, ln_ffn,
           w_ffn_gate, w_ffn_up, w_ffn_down, ln_ple, w_ple_gate, w_ple_proj, ln_final):
    return _forward(x, p, ln_mix, w_in, w_gate_lr, b_gate, gla_norm, w_pool, pool_scale, w_out,
                    ln_ffn, w_ffn_gate, w_ffn_up, w_ffn_down, ln_ple, w_ple_gate, w_ple_proj,
                    ln_final)
```

```python
import functools
import math

import jax
import jax.numpy as jnp
from jax import lax
from jax.experimental import pallas as pl
from jax.experimental.pallas import tpu as pltpu

D_MODEL = 1024
D_PLE = 256
GLA_HEADS = 4
GLA_DK = 64
GLA_DV = 128
GLA_KEY = GLA_HEADS * GLA_DK
GLA_VAL = GLA_HEADS * GLA_DV
GLA_GATE_RANK = 16
GLA_GATE_NORM = 16.0
GLA_CHUNK = 64
POOL_WINDOWS = (2, 4, 8, 16)
POOL_GC = 128
POOL_WIDTH = 512
D_FF = 2816
EPS = 1e-6

LANES = 128
SUBLANES = 8
KEY_TILES = GLA_KEY // LANES
HEADS_PER_TILE = LANES // GLA_DK
SUB_BLOCK = 16
N_SUB = GLA_CHUNK // SUB_BLOCK
N_CAT = SUB_BLOCK * (N_SUB * (N_SUB - 1) // 2)
TRIL_ROWS = 128
DIAG_ROWS = 256
PROJ_COLS = 256
PROJECT_AFTER_GROUP = (3, 7, 11, 23)
PROJECT_AFTER_SCORES = (3, 7)
PROJECT_AFTER_OUTPUTS = (3,)
POOL_HALO = max(POOL_WINDOWS)
FF_CHUNK = 256
FFN_EPILOGUE_AFTER_CHUNK = (2, 5, 8)
TM_MIX = 512
TM_FFN = 1024
VMEM_LIMIT = 56 * 1024 * 1024
LOG2E = math.log2(math.e)

F32 = jnp.float32
BF16 = jnp.bfloat16
NT = (((1,), (1,)), ((), ()))
TN = (((0,), (0,)), ((), ()))


def _dot(a, b, dims=None):
    if dims is None:
        return jnp.dot(a, b, preferred_element_type=F32)
    return lax.dot_general(a, b, dims, preferred_element_type=F32)


def _rmsnorm(x, g):
    return x * lax.rsqrt(jnp.mean(x * x, axis=-1, keepdims=True) + EPS) * g


def _row_bcast(x, row, n):
    return jnp.broadcast_to(x[row:row + 1, :], (n, x.shape[1]))


def _const_spec(shape):
    nd = len(shape)
    return pl.BlockSpec(shape, lambda *_: (0,) * nd, pipeline_mode=pl.Buffered(1))


def _projection_steps(x_ref, ln_ref, wmain_ref, wglr_ref, wgate_ref, bgate_ref,
                      qkg_out, vg_out, u_out):
    hold = {}

    def norm():
        hold["a"] = _rmsnorm(x_ref[...], ln_ref[...]).astype(BF16)

    def piece(col):
        def run():
            z = _dot(hold["a"], wmain_ref[:, col:col + PROJ_COLS])
            for i in range(PROJ_COLS // LANES):
                zi = z[:, i * LANES:(i + 1) * LANES]
                c = col + i * LANES
                if c < GLA_KEY:
                    qkg_out[c // LANES] = zi * (GLA_DK ** -0.5)
                elif c < 2 * GLA_KEY:
                    qkg_out[c // LANES] = zi
                elif c < 2 * GLA_KEY + 2 * GLA_VAL:
                    vg_out[:, c - 2 * GLA_KEY:c - 2 * GLA_KEY + LANES] = zi.astype(BF16)
                else:
                    c -= 2 * GLA_KEY + 2 * GLA_VAL
                    u_out[:, c:c + LANES] = zi
        return run

    def gate():
        g_lr = _dot(hold["a"], wglr_ref[...])
        pre = _dot(g_lr.astype(BF16), wgate_ref[...]) + bgate_ref[...]
        gk = (jnp.minimum(pre, 0.0) - jnp.log1p(jnp.exp(-jnp.abs(pre)))) * (LOG2E / GLA_GATE_NORM)
        for i in range(KEY_TILES):
            qkg_out[2 * KEY_TILES + i] = gk[:, i * LANES:(i + 1) * LANES]

    n_cols = 2 * GLA_KEY + 2 * GLA_VAL + POOL_WIDTH
    return [norm] + [piece(c) for c in range(0, n_cols, PROJ_COLS)] + [gate]


def _mix_kernel(x_next_ref, x_ref, ln_ref, wmain_ref, wglr_ref, wgate_ref, bgate_ref,
                tril_ref, esel_ref, gnorm_ref, wpool_ref, pscale_ref, wout_ref, h_ref,
                qkg_ref, vg_ref, u_ref, qkg_next_ref, vg_next_ref, u_next_ref, state_ref,
                ubuf_ref, b_ref, w_ref, qoff_ref, o_ref, *, tm, tiles_per_seq):
    n = pl.program_id(0)
    t = n % tiles_per_seq
    n_chunks = tm // GLA_CHUNK
    weights = (ln_ref, wmain_ref, wglr_ref, wgate_ref, bgate_ref)

    @pl.when(n == 0)
    def _():
        for step in _projection_steps(x_ref, *weights, qkg_ref, vg_ref, u_ref):
            step()

    @pl.when(t == 0)
    def _():
        state_ref[...] = jnp.zeros_like(state_ref)
        ubuf_ref[0:POOL_HALO, :] = jnp.zeros((POOL_HALO, POOL_WIDTH), F32)

    pending = _projection_steps(x_next_ref, *weights, qkg_next_ref, vg_next_ref, u_next_ref)

    def project_next(k):
        for _ in range(min(k, len(pending))):
            pending.pop(0)()

    vg = vg_ref
    project_next(1)

    def q_tile(i):
        return qkg_ref.at[i]

    def k_tile(i):
        return qkg_ref.at[KEY_TILES + i]

    gk = jnp.concatenate([qkg_ref[2 * KEY_TILES + i] for i in range(KEY_TILES)], axis=1)
    g_hi = gk.astype(BF16)
    rem = gk - g_hi.astype(F32)
    g_mid = rem.astype(BF16)
    g_lo = (rem - g_mid.astype(F32)).astype(BF16)
    tril = tril_ref[...]
    for r0 in range(0, tm, TRIL_ROWS):
        rs = slice(r0, r0 + TRIL_ROWS)
        b_blk = _dot(tril, g_hi[rs]) + _dot(tril, g_mid[rs]) + _dot(tril, g_lo[rs])
        for i in range(KEY_TILES):
            b_ref[i, rs, :] = b_blk[:, i * LANES:(i + 1) * LANES]

    row8 = lax.broadcasted_iota(jnp.int32, (SUBLANES, LANES), 0)
    neg_inf = jnp.full((SUBLANES, LANES), -jnp.inf, F32)
    zeros8 = jnp.zeros((SUBLANES, LANES), F32)

    def group_body(g):
        r0 = g * SUB_BLOCK
        for i in range(KEY_TILES):
            b_lo = b_ref[i, pl.ds(r0, SUBLANES), :]
            b_hi = b_ref[i, pl.ds(r0 + SUBLANES, SUBLANES), :]
            q_lo = q_tile(i)[pl.ds(r0, SUBLANES), :]
            q_hi = q_tile(i)[pl.ds(r0 + SUBLANES, SUBLANES), :]
            k_lo = k_tile(i)[pl.ds(r0, SUBLANES), :]
            k_hi = k_tile(i)[pl.ds(r0 + SUBLANES, SUBLANES), :]
            for j in range(SUB_BLOCK):
                s = j % SUBLANES
                b_j = _row_bcast(b_lo if j < SUBLANES else b_hi, s, SUBLANES)
                k_j = _row_bcast(k_lo if j < SUBLANES else k_hi, s, SUBLANES)
                if j < SUBLANES:
                    e_lo = jnp.exp2(jnp.where(row8 >= j, b_lo - b_j, neg_inf))
                    e_hi = jnp.exp2(b_hi - b_j)
                    w_lo = q_lo * k_j * e_lo
                else:
                    e_hi = jnp.exp2(jnp.where(row8 >= j - SUBLANES, b_hi - b_j, neg_inf))
                    w_lo = zeros8
                w_hi = q_hi * k_j * e_hi
                col = j * GLA_KEY + i * LANES
                w_ref[pl.ds(r0, SUB_BLOCK), col:col + LANES] = (
                    jnp.concatenate([w_lo, w_hi], axis=0).astype(BF16))
                if j == 0:
                    qoff_ref[pl.ds(r0, SUB_BLOCK), i * LANES:(i + 1) * LANES] = (
                        jnp.concatenate([q_lo * e_lo, q_hi * e_hi], axis=0))
    a_parts = []
    for r0 in range(0, tm, DIAG_ROWS):
        for g in range(r0 // SUB_BLOCK, (r0 + DIAG_ROWS) // SUB_BLOCK):
            group_body(g)
            if g in PROJECT_AFTER_GROUP:
                project_next(1)
        a_parts.append(_dot(w_ref[r0:r0 + DIAG_ROWS, :], esel_ref[...]))
    a_diag = jnp.concatenate(a_parts, axis=0)

    lane_head = (lax.broadcasted_iota(jnp.int32, (tm, GLA_KEY), 1) % LANES) // GLA_DK
    blk_row = (lax.broadcasted_iota(jnp.int32, (tm, GLA_KEY), 0) % GLA_CHUNK) // SUB_BLOCK
    blk_lane = (lax.broadcasted_iota(jnp.int32, (tm, GLA_KEY), 1) % GLA_CHUNK) // SUB_BLOCK
    b_all = jnp.concatenate([b_ref[i] for i in range(KEY_TILES)], axis=1)
    q_all = jnp.concatenate([q_tile(i)[...] for i in range(KEY_TILES)], axis=1)
    q_in_all = q_all * jnp.exp2(b_all)
    q_off_all = qoff_ref[...]
    a_dg, q_in, q_off = [], [], []
    for hh in range(HEADS_PER_TILE):
        sel = lane_head == hh
        a_dg.append(jnp.where(sel & (blk_row == blk_lane), a_diag, 0.0).astype(BF16))
        q_in.append(jnp.where(sel, q_in_all, 0.0).astype(BF16))
        q_off.append(jnp.where(sel, q_off_all, 0.0).astype(BF16))

    prow = lax.broadcasted_iota(jnp.int32, (GLA_CHUNK, N_CAT), 0) // SUB_BLOCK
    pcol = lax.broadcasted_iota(jnp.int32, (GLA_CHUNK, N_CAT), 1)
    pgrp = jnp.zeros_like(pcol)
    start = 0
    for i in range(1, N_SUB):
        pgrp = jnp.where((pcol >= start) & (pcol < start + i * SUB_BLOCK), i, pgrp)
        start += i * SUB_BLOCK
    off_mask = prow == pgrp
    first_head = lax.broadcasted_iota(jnp.int32, (GLA_DV, LANES), 1) < GLA_DK

    p_off = {}
    d_state = {}
    decay = {}
    for c in range(n_chunks):
        r0 = c * GLA_CHUNK
        rs = slice(r0, r0 + GLA_CHUNK)
        for i in range(KEY_TILES):
            bc = b_ref[i, rs, :]
            kc = k_tile(i)[rs, :]
            b_last = _row_bcast(bc, GLA_CHUNK - 1, GLA_CHUNK)
            k_dec = (kc * jnp.exp2(b_last - bc)).astype(BF16)
            decay[c, i] = jnp.exp2(b_last[0:SUBLANES])
            pieces = []
            for s in range(1, N_SUB):
                m = s * SUB_BLOCK
                r_s = _row_bcast(bc, m, m)
                pieces.append(kc[0:m] * jnp.exp2(r_s - bc[0:m]))
            k_cat = jnp.concatenate(pieces, axis=0).astype(BF16)
            halves = []
            for hh in range(HEADS_PER_TILE):
                h = i * HEADS_PER_TILE + hh
                vs = slice(h * GLA_DV, (h + 1) * GLA_DV)
                p = _dot(q_off[hh][rs, i * LANES:(i + 1) * LANES], k_cat, NT)
                p_off[c, h] = jnp.where(off_mask, p, 0.0).astype(BF16)
                halves.append(_dot(vg[rs, vs], k_dec, TN))
            d_state[c, i] = jnp.where(first_head, halves[0], halves[1])
        if c in PROJECT_AFTER_SCORES:
            project_next(1)

    state_in = {}
    for i in range(KEY_TILES):
        ls = slice(i * LANES, (i + 1) * LANES)
        s = state_ref[:, ls]
        for c in range(n_chunks):
            state_in[c, i] = s.astype(BF16)
            s = (s.reshape(GLA_DV // SUBLANES, SUBLANES, LANES) * decay[c, i][None]
                 ).reshape(GLA_DV, LANES) + d_state[c, i]
        state_ref[:, ls] = s

    for c in range(n_chunks):
        r0 = c * GLA_CHUNK
        rs = slice(r0, r0 + GLA_CHUNK)
        for h in range(GLA_HEADS):
            i, hh = divmod(h, HEADS_PER_TILE)
            ls = slice(i * LANES, (i + 1) * LANES)
            vs = slice(h * GLA_DV, (h + 1) * GLA_DV)
            vc = vg[rs, vs]
            v_cat = jnp.concatenate([vc[0:s * SUB_BLOCK] for s in range(1, N_SUB)], axis=0)
            v_two = jnp.concatenate([vc, vc], axis=0)
            o_ref[rs, vs] = (_dot(p_off[c, h], v_cat) + _dot(a_dg[hh][rs, ls], v_two)
                             + _dot(q_in[hh][rs, ls], state_in[c, i], NT))
        if c in PROJECT_AFTER_OUTPUTS:
            project_next(1)

    gated = []
    for h in range(GLA_HEADS):
        vs = slice(h * GLA_DV, (h + 1) * GLA_DV)
        o_h = _rmsnorm(o_ref[:, vs], gnorm_ref[...])
        g_h = vg[:, GLA_VAL + h * GLA_DV:GLA_VAL + (h + 1) * GLA_DV].astype(F32)
        gated.append((o_h * (g_h * jax.nn.sigmoid(g_h))).astype(BF16))
    project_next(len(pending))
    mix = _dot(jnp.concatenate(gated, axis=1), wout_ref[0:GLA_VAL, :])

    u = u_ref[...]
    ubuf_ref[POOL_HALO:POOL_HALO + tm, :] = u
    pos = t * tm + lax.broadcasted_iota(jnp.int32, (tm, 1), 0)
    pooled = []
    for g, w in enumerate(POOL_WINDOWS):
        ls = slice(g * POOL_GC, (g + 1) * POOL_GC)
        acc = ubuf_ref[:, ls]
        lo, step = 0, 1
        while step < w:
            acc = acc[step:] + acc[:-step]
            lo += step
            step *= 2
        acc = acc[POOL_HALO - lo:]
        cnt = jnp.minimum(pos + 1, w).astype(F32)
        y = _dot((acc / cnt - u[:, ls]).astype(BF16), wpool_ref[g])
        pooled.append((y * pscale_ref[:, ls]).astype(BF16))
    ubuf_ref[0:POOL_HALO, :] = ubuf_ref[tm:tm + POOL_HALO, :]
    mix = mix + _dot(jnp.concatenate(pooled, axis=1), wout_ref[GLA_VAL:, :])

    h_ref[...] = x_ref[...] + mix

    qkg_ref[...] = qkg_next_ref[...]
    vg_ref[...] = vg_next_ref[...]
    u_ref[...] = u_next_ref[...]


def _ffn_kernel(h_ref, p_ref, lnf_ref, wg_ref, wu_ref, wd_ref, lnp_ref, wpg_ref, wpp_ref,
                lnfin_ref, out_ref, hid_ref):
    half = h_ref.shape[0] // 2
    first, second = slice(0, half), slice(half, 2 * half)
    chunks = range(0, D_FF, FF_CHUNK)

    def norm_in(rows):
        return _rmsnorm(h_ref[rows, :], lnf_ref[...]).astype(BF16)

    def gate_up(rows, f, c):
        g = _dot(f, wg_ref[:, c:c + FF_CHUNK])
        u = _dot(f, wu_ref[:, c:c + FF_CHUNK])
        hid_ref[rows, c:c + FF_CHUNK] = (g * jax.nn.sigmoid(g) * u).astype(BF16)

    def down(rows):
        return h_ref[rows, :] + _dot(hid_ref[rows, :], wd_ref[...])

    def epilogue_steps(rows, h):
        hold = {}

        def ple_gate():
            e = _rmsnorm(h, lnp_ref[...]).astype(BF16)
            hold["gate"] = jax.nn.sigmoid(_dot(e, wpg_ref[...]))

        def ple_proj():
            hold["pp"] = _dot(p_ref[rows, :].astype(BF16), wpp_ref[...])

        def final():
            out_ref[rows, :] = _rmsnorm(h + hold["gate"] * hold["pp"], lnfin_ref[...])

        return [ple_gate, ple_proj, final]

    f_first = norm_in(first)
    f_second = None
    for ci, c in enumerate(chunks):
        gate_up(first, f_first, c)
        if ci == 1:
            f_second = norm_in(second)
    pending = epilogue_steps(first, down(first))
    for ci, c in enumerate(chunks):
        gate_up(second, f_second, c)
        if ci in FFN_EPILOGUE_AFTER_CHUNK:
            pending.pop(0)()
    for step in epilogue_steps(second, down(second)):
        step()


def _selection_matrix():
    r = jnp.arange(SUB_BLOCK * GLA_KEY)
    j, hd = r // GLA_KEY, r % GLA_KEY
    c = jnp.arange(GLA_KEY)
    same_head = (hd // GLA_DK)[:, None] == (c // GLA_CHUNK)[None, :]
    same_col = j[:, None] == (c % SUB_BLOCK)[None, :]
    return (same_head & same_col).astype(BF16)


def _chunk_tril():
    r = jnp.arange(TRIL_ROWS)
    same_chunk = (r // GLA_CHUNK)[:, None] == (r // GLA_CHUNK)[None, :]
    return (same_chunk & (r[None, :] <= r[:, None])).astype(BF16)


@jax.jit
def _forward(x, p, ln_mix, w_in, w_gate_lr, b_gate, gla_norm, w_pool, pool_scale, w_out,
             ln_ffn, w_ffn_gate, w_ffn_up, w_ffn_down, ln_ple, w_ple_gate, w_ple_proj,
             ln_final):
    bsz, seq, _ = x.shape
    n_tok = bsz * seq
    h = x.reshape(n_tok, D_MODEL)
    assert ln_mix.shape[0] == 1
    s_glr = 2 * GLA_KEY + 2 * GLA_VAL
    w_main = jnp.concatenate(
        [w_in[0][:, :s_glr], w_in[0][:, s_glr + GLA_GATE_RANK:]], axis=1).astype(BF16)
    w_glr = jnp.pad(w_in[0][:, s_glr:s_glr + GLA_GATE_RANK],
                    ((0, 0), (0, LANES - GLA_GATE_RANK))).astype(BF16)
    w_gate = jnp.pad(w_gate_lr[0], ((0, LANES - GLA_GATE_RANK), (0, 0))).astype(BF16)

    tm = TM_MIX
    n_tiles = n_tok // tm
    h = pl.pallas_call(
        functools.partial(_mix_kernel, tm=tm, tiles_per_seq=seq // tm),
        grid=(n_tiles,),
        in_specs=[
            pl.BlockSpec((tm, D_MODEL), lambda n: (jnp.minimum(n + 1, n_tiles - 1), 0)),
            pl.BlockSpec((tm, D_MODEL), lambda n: (n, 0)),
            _const_spec((1, D_MODEL)),
            _const_spec((D_MODEL, 2 * GLA_KEY + 2 * GLA_VAL + POOL_WIDTH)),
            _const_spec((D_MODEL, LANES)),
            _const_spec((LANES, GLA_KEY)),
            _const_spec((1, GLA_KEY)),
            _const_spec((TRIL_ROWS, TRIL_ROWS)),
            _const_spec((SUB_BLOCK * GLA_KEY, GLA_KEY)),
            _const_spec((1, GLA_DV)),
            _const_spec((len(POOL_WINDOWS), POOL_GC, POOL_GC)),
            _const_spec((1, POOL_WIDTH)),
            _const_spec((D_MODEL, D_MODEL)),
        ],
        out_specs=pl.BlockSpec((tm, D_MODEL), lambda n: (n, 0)),
        out_shape=jax.ShapeDtypeStruct((n_tok, D_MODEL), F32),
        scratch_shapes=[
            pltpu.VMEM((3 * KEY_TILES, tm, LANES), F32),
            pltpu.VMEM((tm, 2 * GLA_VAL), BF16),
            pltpu.VMEM((tm, POOL_WIDTH), F32),
            pltpu.VMEM((3 * KEY_TILES, tm, LANES), F32),
            pltpu.VMEM((tm, 2 * GLA_VAL), BF16),
            pltpu.VMEM((tm, POOL_WIDTH), F32),
            pltpu.VMEM((GLA_DV, GLA_KEY), F32),
            pltpu.VMEM((POOL_HALO + tm, POOL_WIDTH), F32),
            pltpu.VMEM((KEY_TILES, tm, LANES), F32),
            pltpu.VMEM((tm, SUB_BLOCK * GLA_KEY), BF16),
            pltpu.VMEM((tm, GLA_KEY), F32),
            pltpu.VMEM((tm, GLA_VAL), F32),
        ],
        compiler_params=pltpu.CompilerParams(
            dimension_semantics=("arbitrary",), vmem_limit_bytes=VMEM_LIMIT),
        name="mix",
    )(h, h, ln_mix[0][None], w_main, w_glr, w_gate, b_gate[0][None], _chunk_tril(),
      _selection_matrix(), gla_norm[0][None], w_pool[0].astype(BF16), pool_scale[0][None],
      w_out[0].astype(BF16))

    h = pl.pallas_call(
        _ffn_kernel,
        grid=(n_tok // TM_FFN,),
        in_specs=[
            pl.BlockSpec((TM_FFN, D_MODEL), lambda r: (r, 0)),
            pl.BlockSpec((TM_FFN, D_PLE), lambda r: (r, 0)),
            _const_spec((1, D_MODEL)),
            _const_spec((D_MODEL, D_FF)),
            _const_spec((D_MODEL, D_FF)),
            _const_spec((D_FF, D_MODEL)),
            _const_spec((1, D_MODEL)),
            _const_spec((D_MODEL, D_MODEL)),
            _const_spec((D_PLE, D_MODEL)),
            _const_spec((1, D_MODEL)),
        ],
        out_specs=pl.BlockSpec((TM_FFN, D_MODEL), lambda r: (r, 0)),
        out_shape=jax.ShapeDtypeStruct((n_tok, D_MODEL), F32),
        scratch_shapes=[pltpu.VMEM((TM_FFN, D_FF), BF16)],
        compiler_params=pltpu.CompilerParams(
            dimension_semantics=("arbitrary",), vmem_limit_bytes=VMEM_LIMIT),
        name="ffn",
    )(h, p[0].reshape(n_tok, D_PLE), ln_ffn[0][None], w_ffn_gate[0].astype(BF16),
      w_ffn_up[0].astype(BF16), w_ffn_down[0].astype(BF16), ln_ple[0][None],
      w_ple_gate[0].astype(BF16), w_ple_proj[0].astype(BF16), ln_final[None])
    return h.reshape(bsz, seq, D_MODEL)


def kernel(x, p, ln_mix, w_in, w_gate_lr, b_gate, gla_norm, w_pool, pool_scale, w_out, ln_ffn,
           w_ffn_gate, w_ffn_up, w_ffn_down, ln_ple, w_ple_gate, w_ple_proj, ln_final):
    return _forward(x, p, ln_mix, w_in, w_gate_lr, b_gate, gla_norm, w_pool, pool_scale, w_out,
                    ln_ffn, w_ffn_gate, w_ffn_up, w_ffn_down, ln_ple, w_ple_gate, w_ple_proj,
                    ln_final)
```

```python
import functools
import math

import jax
import jax.numpy as jnp
from jax import lax
from jax.experimental import pallas as pl
from jax.experimental.pallas import tpu as pltpu

D_MODEL = 1024
D_PLE = 256
GLA_HEADS = 4
GLA_DK = 64
GLA_DV = 128
GLA_KEY = GLA_HEADS * GLA_DK
GLA_VAL = GLA_HEADS * GLA_DV
GLA_GATE_RANK = 16
GLA_GATE_NORM = 16.0
GLA_CHUNK = 64
POOL_WINDOWS = (2, 4, 8, 16)
POOL_GC = 128
POOL_WIDTH = 512
D_FF = 2816
EPS = 1e-6

LANES = 128
SUBLANES = 8
KEY_TILES = GLA_KEY // LANES
HEADS_PER_TILE = LANES // GLA_DK
SUB_BLOCK = 16
N_SUB = GLA_CHUNK // SUB_BLOCK
N_CAT = SUB_BLOCK * (N_SUB * (N_SUB - 1) // 2)
TRIL_ROWS = 128
DIAG_ROWS = 256
PROJ_COLS = 256
PROJECT_AFTER_GROUP = (3, 7, 11, 23)
PROJECT_AFTER_SCORES = (3, 7)
PROJECT_AFTER_OUTPUTS = (3,)
POOL_HALO = max(POOL_WINDOWS)
FF_CHUNK = 256
TM_MIX = 512
TM_FFN = 1024
VMEM_LIMIT = 56 * 1024 * 1024
LOG2E = math.log2(math.e)

F32 = jnp.float32
BF16 = jnp.bfloat16
NT = (((1,), (1,)), ((), ()))
TN = (((0,), (0,)), ((), ()))


def _dot(a, b, dims=None):
    if dims is None:
        return jnp.dot(a, b, preferred_element_type=F32)
    return lax.dot_general(a, b, dims, preferred_element_type=F32)


def _rmsnorm(x, g):
    return x * lax.rsqrt(jnp.mean(x * x, axis=-1, keepdims=True) + EPS) * g


def _row_bcast(x, row, n):
    return jnp.broadcast_to(x[row:row + 1, :], (n, x.shape[1]))


def _const_spec(shape):
    nd = len(shape)
    return pl.BlockSpec(shape, lambda *_: (0,) * nd, pipeline_mode=pl.Buffered(1))


def _projection_steps(x_ref, ln_ref, wmain_ref, wglr_ref, wgate_ref, bgate_ref,
                      qkg_out, vg_out, u_out):
    hold = {}

    def norm():
        hold["a"] = _rmsnorm(x_ref[...], ln_ref[...]).astype(BF16)

    def piece(col):
        def run():
            z = _dot(hold["a"], wmain_ref[:, col:col + PROJ_COLS])
            for i in range(PROJ_COLS // LANES):
                zi = z[:, i * LANES:(i + 1) * LANES]
                c = col + i * LANES
                if c < GLA_KEY:
                    qkg_out[c // LANES] = zi * (GLA_DK ** -0.5)
                elif c < 2 * GLA_KEY:
                    qkg_out[c // LANES] = zi
                elif c < 2 * GLA_KEY + 2 * GLA_VAL:
                    vg_out[:, c - 2 * GLA_KEY:c - 2 * GLA_KEY + LANES] = zi.astype(BF16)
                else:
                    c -= 2 * GLA_KEY + 2 * GLA_VAL
                    u_out[:, c:c + LANES] = zi
        return run

    def gate():
        g_lr = _dot(hold["a"], wglr_ref[...])
        pre = _dot(g_lr.astype(BF16), wgate_ref[...]) + bgate_ref[...]
        gk = (jnp.minimum(pre, 0.0) - jnp.log1p(jnp.exp(-jnp.abs(pre)))) * (LOG2E / GLA_GATE_NORM)
        for i in range(KEY_TILES):
            qkg_out[2 * KEY_TILES + i] = gk[:, i * LANES:(i + 1) * LANES]

    n_cols = 2 * GLA_KEY + 2 * GLA_VAL + POOL_WIDTH
    return [norm] + [piece(c) for c in range(0, n_cols, PROJ_COLS)] + [gate]


def _mix_kernel(x_next_ref, x_ref, ln_ref, wmain_ref, wglr_ref, wgate_ref, bgate_ref,
                tril_ref, esel_ref, gnorm_ref, wpool_ref, pscale_ref, wout_ref, h_ref,
                qkg_ref, vg_ref, u_ref, qkg_next_ref, vg_next_ref, u_next_ref, state_ref,
                ubuf_ref, b_ref, w_ref, qoff_ref, o_ref, *, tm, tiles_per_seq):
    n = pl.program_id(0)
    t = n % tiles_per_seq
    n_chunks = tm // GLA_CHUNK
    weights = (ln_ref, wmain_ref, wglr_ref, wgate_ref, bgate_ref)

    @pl.when(n == 0)
    def _():
        for step in _projection_steps(x_ref, *weights, qkg_ref, vg_ref, u_ref):
            step()

    @pl.when(t == 0)
    def _():
        state_ref[...] = jnp.zeros_like(state_ref)
        ubuf_ref[0:POOL_HALO, :] = jnp.zeros((POOL_HALO, POOL_WIDTH), F32)

    pending = _projection_steps(x_next_ref, *weights, qkg_next_ref, vg_next_ref, u_next_ref)

    def project_next(k):
        for _ in range(min(k, len(pending))):
            pending.pop(0)()

    vg = vg_ref
    project_next(1)

    def q_tile(i):
        return qkg_ref.at[i]

    def k_tile(i):
        return qkg_ref.at[KEY_TILES + i]

    gk = jnp.concatenate([qkg_ref[2 * KEY_TILES + i] for i in range(KEY_TILES)], axis=1)
    g_hi = gk.astype(BF16)
    rem = gk - g_hi.astype(F32)
    g_mid = rem.astype(BF16)
    g_lo = (rem - g_mid.astype(F32)).astype(BF16)
    tril = tril_ref[...]
    for r0 in range(0, tm, TRIL_ROWS):
        rs = slice(r0, r0 + TRIL_ROWS)
        b_blk = _dot(tril, g_hi[rs]) + _dot(tril, g_mid[rs]) + _dot(tril, g_lo[rs])
        for i in range(KEY_TILES):
            b_ref[i, rs, :] = b_blk[:, i * LANES:(i + 1) * LANES]

    row8 = lax.broadcasted_iota(jnp.int32, (SUBLANES, LANES), 0)
    neg_inf = jnp.full((SUBLANES, LANES), -jnp.inf, F32)
    zeros8 = jnp.zeros((SUBLANES, LANES), F32)

    def group_body(g):
        r0 = g * SUB_BLOCK
        for i in range(KEY_TILES):
            b_lo = b_ref[i, pl.ds(r0, SUBLANES), :]
            b_hi = b_ref[i, pl.ds(r0 + SUBLANES, SUBLANES), :]
            q_lo = q_tile(i)[pl.ds(r0, SUBLANES), :]
            q_hi = q_tile(i)[pl.ds(r0 + SUBLANES, SUBLANES), :]
            k_lo = k_tile(i)[pl.ds(r0, SUBLANES), :]
            k_hi = k_tile(i)[pl.ds(r0 + SUBLANES, SUBLANES), :]
            for j in range(SUB_BLOCK):
                s = j % SUBLANES
                b_j = _row_bcast(b_lo if j < SUBLANES else b_hi, s, SUBLANES)
                k_j = _row_bcast(k_lo if j < SUBLANES else k_hi, s, SUBLANES)
                if j < SUBLANES:
                    e_lo = jnp.exp2(jnp.where(row8 >= j, b_lo - b_j, neg_inf))
                    e_hi = jnp.exp2(b_hi - b_j)
                    w_lo = q_lo * k_j * e_lo
                else:
                    e_hi = jnp.exp2(jnp.where(row8 >= j - SUBLANES, b_hi - b_j, neg_inf))
                    w_lo = zeros8
                w_hi = q_hi * k_j * e_hi
                col = j * GLA_KEY + i * LANES
                w_ref[pl.ds(r0, SUB_BLOCK), col:col + LANES] = (
                    jnp.concatenate([w_lo, w_hi], axis=0).astype(BF16))
                if j == 0:
                    qoff_ref[pl.ds(r0, SUB_BLOCK), i * LANES:(i + 1) * LANES] = (
                        jnp.concatenate([q_lo * e_lo, q_hi * e_hi], axis=0))
    a_parts = []
    for r0 in range(0, tm, DIAG_ROWS):
        for g in range(r0 // SUB_BLOCK, (r0 + DIAG_ROWS) // SUB_BLOCK):
            group_body(g)
            if g in PROJECT_AFTER_GROUP:
                project_next(1)
        a_parts.append(_dot(w_ref[r0:r0 + DIAG_ROWS, :], esel_ref[...]))
    a_diag = jnp.concatenate(a_parts, axis=0)

    lane_head = (lax.broadcasted_iota(jnp.int32, (tm, GLA_KEY), 1) % LANES) // GLA_DK
    blk_row = (lax.broadcasted_iota(jnp.int32, (tm, GLA_KEY), 0) % GLA_CHUNK) // SUB_BLOCK
    blk_lane = (lax.broadcasted_iota(jnp.int32, (tm, GLA_KEY), 1) % GLA_CHUNK) // SUB_BLOCK
    b_all = jnp.concatenate([b_ref[i] for i in range(KEY_TILES)], axis=1)
    q_all = jnp.concatenate([q_tile(i)[...] for i in range(KEY_TILES)], axis=1)
    q_in = (q_all * jnp.exp2(b_all)).astype(BF16)
    a_dg = jnp.where(blk_row == blk_lane, a_diag, 0.0).astype(BF16)
    q_off_all = qoff_ref[...]
    q_off = [jnp.where(lane_head == hh, q_off_all, 0.0).astype(BF16)
             for hh in range(HEADS_PER_TILE)]

    stacked = HEADS_PER_TILE * GLA_CHUNK
    prow = (lax.broadcasted_iota(jnp.int32, (stacked, N_CAT), 0) % GLA_CHUNK) // SUB_BLOCK
    pcol = lax.broadcasted_iota(jnp.int32, (stacked, N_CAT), 1)
    pgrp = jnp.zeros_like(pcol)
    start = 0
    for i in range(1, N_SUB):
        pgrp = jnp.where((pcol >= start) & (pcol < start + i * SUB_BLOCK), i, pgrp)
        start += i * SUB_BLOCK
    off_mask = prow == pgrp
    first_head_lanes = lax.broadcasted_iota(jnp.int32, (GLA_DV, LANES), 1) < GLA_DK
    first_head_rows = lax.broadcasted_iota(jnp.int32, (LANES, GLA_DV), 0) < GLA_DK
    zeros_v = jnp.zeros((GLA_CHUNK, GLA_DV), BF16)

    p_off = {}
    d_state = {}
    decay = {}
    for c in range(n_chunks):
        r0 = c * GLA_CHUNK
        rs = slice(r0, r0 + GLA_CHUNK)
        for i in range(KEY_TILES):
            ls = slice(i * LANES, (i + 1) * LANES)
            bc = b_ref[i, rs, :]
            kc = k_tile(i)[rs, :]
            b_last = _row_bcast(bc, GLA_CHUNK - 1, GLA_CHUNK)
            k_dec = (kc * jnp.exp2(b_last - bc)).astype(BF16)
            decay[c, i] = jnp.exp2(b_last[0:SUBLANES])
            pieces = []
            for s in range(1, N_SUB):
                m = s * SUB_BLOCK
                r_s = _row_bcast(bc, m, m)
                pieces.append(kc[0:m] * jnp.exp2(r_s - bc[0:m]))
            k_cat = jnp.concatenate(pieces, axis=0).astype(BF16)
            q_two = jnp.concatenate([q[rs, ls] for q in q_off], axis=0)
            p_off[c, i] = jnp.where(off_mask, _dot(q_two, k_cat, NT), 0.0).astype(BF16)
            ds = _dot(vg[rs, i * HEADS_PER_TILE * GLA_DV:(i + 1) * HEADS_PER_TILE * GLA_DV],
                      k_dec, TN)
            d_state[c, i] = jnp.where(first_head_lanes, ds[0:GLA_DV], ds[GLA_DV:])
        if c in PROJECT_AFTER_SCORES:
            project_next(1)

    state_in = {}
    for i in range(KEY_TILES):
        ls = slice(i * LANES, (i + 1) * LANES)
        s = state_ref[:, ls]
        for c in range(n_chunks):
            s_t = s.T
            state_in[c, i] = jnp.concatenate(
                [jnp.where(first_head_rows, s_t, 0.0), jnp.where(first_head_rows, 0.0, s_t)],
                axis=1).astype(BF16)
            s = (s.reshape(GLA_DV // SUBLANES, SUBLANES, LANES) * decay[c, i][None]
                 ).reshape(GLA_DV, LANES) + d_state[c, i]
        state_ref[:, ls] = s

    for c in range(n_chunks):
        r0 = c * GLA_CHUNK
        rs = slice(r0, r0 + GLA_CHUNK)
        for i in range(KEY_TILES):
            ls = slice(i * LANES, (i + 1) * LANES)
            vs = slice(i * HEADS_PER_TILE * GLA_DV, (i + 1) * HEADS_PER_TILE * GLA_DV)
            v_heads = [vg[rs, (i * HEADS_PER_TILE + hh) * GLA_DV:(i * HEADS_PER_TILE + hh + 1) * GLA_DV]
                       for hh in range(HEADS_PER_TILE)]
            off = []
            for hh, vc in enumerate(v_heads):
                v_cat = jnp.concatenate([vc[0:s * SUB_BLOCK] for s in range(1, N_SUB)], axis=0)
                off.append(_dot(p_off[c, i][hh * GLA_CHUNK:(hh + 1) * GLA_CHUNK], v_cat))
            v_diag = jnp.concatenate(
                [jnp.concatenate([v_heads[0], zeros_v], axis=1),
                 jnp.concatenate([zeros_v, v_heads[1]], axis=1)], axis=0)
            o_ref[rs, vs] = (jnp.concatenate(off, axis=1) + _dot(a_dg[rs, ls], v_diag)
                             + _dot(q_in[rs, ls], state_in[c, i]))
        if c in PROJECT_AFTER_OUTPUTS:
            project_next(1)

    gated = []
    for h in range(GLA_HEADS):
        vs = slice(h * GLA_DV, (h + 1) * GLA_DV)
        o_h = _rmsnorm(o_ref[:, vs], gnorm_ref[...])
        g_h = vg[:, GLA_VAL + h * GLA_DV:GLA_VAL + (h + 1) * GLA_DV].astype(F32)
        gated.append((o_h * (g_h * jax.nn.sigmoid(g_h))).astype(BF16))
    project_next(len(pending))
    mix = _dot(jnp.concatenate(gated, axis=1), wout_ref[0:GLA_VAL, :])

    u = u_ref[...]
    ubuf_ref[POOL_HALO:POOL_HALO + tm, :] = u
    pos = t * tm + lax.broadcasted_iota(jnp.int32, (tm, 1), 0)
    pooled = []
    for g, w in enumerate(POOL_WINDOWS):
        ls = slice(g * POOL_GC, (g + 1) * POOL_GC)
        acc = ubuf_ref[:, ls]
        lo, step = 0, 1
        while step < w:
            acc = acc[step:] + acc[:-step]
            lo += step
            step *= 2
        acc = acc[POOL_HALO - lo:]
        cnt = jnp.minimum(pos + 1, w).astype(F32)
        y = _dot((acc / cnt - u[:, ls]).astype(BF16), wpool_ref[g])
        pooled.append((y * pscale_ref[:, ls]).astype(BF16))
    ubuf_ref[0:POOL_HALO, :] = ubuf_ref[tm:tm + POOL_HALO, :]
    mix = mix + _dot(jnp.concatenate(pooled, axis=1), wout_ref[GLA_VAL:, :])

    h_ref[...] = x_ref[...] + mix

    qkg_ref[...] = qkg_next_ref[...]
    vg_ref[...] = vg_next_ref[...]
    u_ref[...] = u_next_ref[...]


def _ffn_kernel(h_ref, p_ref, lnf_ref, wg_ref, wu_ref, wd_ref, lnp_ref, wpg_ref, wpp_ref,
                lnfin_ref, out_ref, hid_ref):
    h = h_ref[...]
    f = _rmsnorm(h, lnf_ref[...]).astype(BF16)
    for c in range(0, D_FF, FF_CHUNK):
        g = _dot(f, wg_ref[:, c:c + FF_CHUNK])
        u = _dot(f, wu_ref[:, c:c + FF_CHUNK])
        hid_ref[:, c:c + FF_CHUNK] = (g * jax.nn.sigmoid(g) * u).astype(BF16)
    h = h + _dot(hid_ref[...], wd_ref[...])
    e = _rmsnorm(h, lnp_ref[...]).astype(BF16)
    gate = jax.nn.sigmoid(_dot(e, wpg_ref[...]))
    h = h + gate * _dot(p_ref[...].astype(BF16), wpp_ref[...])
    out_ref[...] = _rmsnorm(h, lnfin_ref[...])


def _selection_matrix():
    r = jnp.arange(SUB_BLOCK * GLA_KEY)
    j, hd = r // GLA_KEY, r % GLA_KEY
    c = jnp.arange(GLA_KEY)
    same_head = (hd // GLA_DK)[:, None] == (c // GLA_CHUNK)[None, :]
    same_col = j[:, None] == (c % SUB_BLOCK)[None, :]
    return (same_head & same_col).astype(BF16)


def _chunk_tril():
    r = jnp.arange(TRIL_ROWS)
    same_chunk = (r // GLA_CHUNK)[:, None] == (r // GLA_CHUNK)[None, :]
    return (same_chunk & (r[None, :] <= r[:, None])).astype(BF16)


@jax.jit
def _forward(x, p, ln_mix, w_in, w_gate_lr, b_gate, gla_norm, w_pool, pool_scale, w_out,
             ln_ffn, w_ffn_gate, w_ffn_up, w_ffn_down, ln_ple, w_ple_gate, w_ple_proj,
             ln_final):
    bsz, seq, _ = x.shape
    n_tok = bsz * seq
    h = x.reshape(n_tok, D_MODEL)
    assert ln_mix.shape[0] == 1
    s_glr = 2 * GLA_KEY + 2 * GLA_VAL
    w_main = jnp.concatenate(
        [w_in[0][:, :s_glr], w_in[0][:, s_glr + GLA_GATE_RANK:]], axis=1).astype(BF16)
    w_glr = jnp.pad(w_in[0][:, s_glr:s_glr + GLA_GATE_RANK],
                    ((0, 0), (0, LANES - GLA_GATE_RANK))).astype(BF16)
    w_gate = jnp.pad(w_gate_lr[0], ((0, LANES - GLA_GATE_RANK), (0, 0))).astype(BF16)

    tm = TM_MIX
    n_tiles = n_tok // tm
    h = pl.pallas_call(
        functools.partial(_mix_kernel, tm=tm, tiles_per_seq=seq // tm),
        grid=(n_tiles,),
        in_specs=[
            pl.BlockSpec((tm, D_MODEL), lambda n: (jnp.minimum(n + 1, n_tiles - 1), 0)),
            pl.BlockSpec((tm, D_MODEL), lambda n: (n, 0)),
            _const_spec((1, D_MODEL)),
            _const_spec((D_MODEL, 2 * GLA_KEY + 2 * GLA_VAL + POOL_WIDTH)),
            _const_spec((D_MODEL, LANES)),
            _const_spec((LANES, GLA_KEY)),
            _const_spec((1, GLA_KEY)),
            _const_spec((TRIL_ROWS, TRIL_ROWS)),
            _const_spec((SUB_BLOCK * GLA_KEY, GLA_KEY)),
            _const_spec((1, GLA_DV)),
            _const_spec((len(POOL_WINDOWS), POOL_GC, POOL_GC)),
            _const_spec((1, POOL_WIDTH)),
            _const_spec((D_MODEL, D_MODEL)),
        ],
        out_specs=pl.BlockSpec((tm, D_MODEL), lambda n: (n, 0)),
        out_shape=jax.ShapeDtypeStruct((n_tok, D_MODEL), F32),
        scratch_shapes=[
            pltpu.VMEM((3 * KEY_TILES, tm, LANES), F32),
            pltpu.VMEM((tm, 2 * GLA_VAL), BF16),
            pltpu.VMEM((tm, POOL_WIDTH), F32),
            pltpu.VMEM((3 * KEY_TILES, tm, LANES), F32),
            pltpu.VMEM((tm, 2 * GLA_VAL), BF16),
            pltpu.VMEM((tm, POOL_WIDTH), F32),
            pltpu.VMEM((GLA_DV, GLA_KEY), F32),
            pltpu.VMEM((POOL_HALO + tm, POOL_WIDTH), F32),
            pltpu.VMEM((KEY_TILES, tm, LANES), F32),
            pltpu.VMEM((tm, SUB_BLOCK * GLA_KEY), BF16),
            pltpu.VMEM((tm, GLA_KEY), F32),
            pltpu.VMEM((tm, GLA_VAL), F32),
        ],
        compiler_params=pltpu.CompilerParams(
            dimension_semantics=("arbitrary",), vmem_limit_bytes=VMEM_LIMIT),
        name="mix",
    )(h, h, ln_mix[0][None], w_main, w_glr, w_gate, b_gate[0][None], _chunk_tril(),
      _selection_matrix(), gla_norm[0][None], w_pool[0].astype(BF16), pool_scale[0][None],
      w_out[0].astype(BF16))

    h = pl.pallas_call(
        _ffn_kernel,
        grid=(n_tok // TM_FFN,),
        in_specs=[
            pl.BlockSpec((TM_FFN, D_MODEL), lambda r: (r, 0)),
            pl.BlockSpec((TM_FFN, D_PLE), lambda r: (r, 0)),
            _const_spec((1, D_MODEL)),
            _const_spec((D_MODEL, D_FF)),
            _const_spec((D_MODEL, D_FF)),
            _const_spec((D_FF, D_MODEL)),
            _const_spec((1, D_MODEL)),
            _const_spec((D_MODEL, D_MODEL)),
            _const_spec((D_PLE, D_MODEL)),
            _const_spec((1, D_MODEL)),
        ],
        out_specs=pl.BlockSpec((TM_FFN, D_MODEL), lambda r: (r, 0)),
        out_shape=jax.ShapeDtypeStruct((n_tok, D_MODEL), F32),
        scratch_shapes=[pltpu.VMEM((TM_FFN, D_FF), BF16)],
        compiler_params=pltpu.CompilerParams(
            dimension_semantics=("arbitrary",), vmem_limit_bytes=VMEM_LIMIT),
        name="ffn",
    )(h, p[0].reshape(n_tok, D_PLE), ln_ffn[0][None], w_ffn_gate[0].astype(BF16),
      w_ffn_up[0].astype(BF16), w_ffn_down[0].astype(BF16), ln_ple[0][None],
      w_ple_gate[0].astype(BF16), w_ple_proj[0].astype(BF16), ln_final[None])
    return h.reshape(bsz, seq, D_MODEL)


def kernel(x, p, ln_mix, w_in, w_gate_lr, b_gate, gla_norm, w_pool, pool_scale, w_out, ln_ffn,
           w_ffn_gate, w_ffn_up, w_ffn_down, ln_ple, w_ple_gate, w_ple_proj, ln_final):
    return _forward(x, p, ln_mix, w_in, w_gate_lr, b_gate, gla_norm, w_pool, pool_scale, w_out,
                    ln_ffn, w_ffn_gate, w_ffn_up, w_ffn_down, ln_ple, w_ple_gate, w_ple_proj,
                    ln_final)
```

```python
import functools
import math

import jax
import jax.numpy as jnp
import numpy as np
from jax import lax
from jax.experimental import pallas as pl
from jax.experimental.pallas import tpu as pltpu

D_MODEL = 1024
D_PLE = 256
GLA_HEADS = 4
GLA_DK = 64
GLA_DV = 128
GLA_KEY = GLA_HEADS * GLA_DK
GLA_VAL = GLA_HEADS * GLA_DV
GLA_GATE_RANK = 16
GLA_GATE_NORM = 16.0
GLA_CHUNK = 64
POOL_WINDOWS = (2, 4, 8, 16)
POOL_GC = 128
POOL_WIDTH = 512
D_FF = 2816
EPS = 1e-6

LANES = 128
SUBLANES = 8
KEY_TILES = GLA_KEY // LANES
HEADS_PER_TILE = LANES // GLA_DK
SUB_BLOCK = 16
N_SUB = GLA_CHUNK // SUB_BLOCK
N_CAT = SUB_BLOCK * (N_SUB * (N_SUB - 1) // 2)
TRIL_ROWS = 128
DIAG_ROWS = 256
PROJ_COLS = 256
PROJECT_AFTER_GROUP = (3, 7, 11, 23)
PROJECT_AFTER_SCORES = (3, 7)
PROJECT_AFTER_OUTPUTS = (3,)
POOL_HALO = max(POOL_WINDOWS)
FF_CHUNK = 256
TM_MIX = 512
TM_FFN = 1024
VMEM_LIMIT = 56 * 1024 * 1024
LOG2E = math.log2(math.e)

F32 = jnp.float32
BF16 = jnp.bfloat16
NT = (((1,), (1,)), ((), ()))
TN = (((0,), (0,)), ((), ()))


def _dot(a, b, dims=None):
    if dims is None:
        return jnp.dot(a, b, preferred_element_type=F32)
    return lax.dot_general(a, b, dims, preferred_element_type=F32)


def _rmsnorm(x, g):
    return x * lax.rsqrt(jnp.mean(x * x, axis=-1, keepdims=True) + EPS) * g


def _row_bcast(x, row, n):
    return jnp.broadcast_to(x[row:row + 1, :], (n, x.shape[1]))


def _const_spec(shape):
    nd = len(shape)
    return pl.BlockSpec(shape, lambda *_: (0,) * nd, pipeline_mode=pl.Buffered(1))


def _projection_steps(x_ref, ln_ref, wmain_ref, wglr_ref, wgate_ref, bgate_ref,
                      qkg_out, vg_out, u_out):
    hold = {}

    def norm():
        hold["a"] = _rmsnorm(x_ref[...], ln_ref[...]).astype(BF16)

    def piece(col):
        def run():
            z = _dot(hold["a"], wmain_ref[:, col:col + PROJ_COLS])
            for i in range(PROJ_COLS // LANES):
                zi = z[:, i * LANES:(i + 1) * LANES]
                c = col + i * LANES
                if c < GLA_KEY:
                    qkg_out[c // LANES] = zi * (GLA_DK ** -0.5)
                elif c < 2 * GLA_KEY:
                    qkg_out[c // LANES] = zi
                elif c < 2 * GLA_KEY + 2 * GLA_VAL:
                    vg_out[:, c - 2 * GLA_KEY:c - 2 * GLA_KEY + LANES] = zi.astype(BF16)
                else:
                    c -= 2 * GLA_KEY + 2 * GLA_VAL
                    u_out[:, c:c + LANES] = zi
        return run

    def gate():
        g_lr = _dot(hold["a"], wglr_ref[...])
        pre = _dot(g_lr.astype(BF16), wgate_ref[...]) + bgate_ref[...]
        gk = (jnp.minimum(pre, 0.0) - jnp.log1p(jnp.exp(-jnp.abs(pre)))) * (LOG2E / GLA_GATE_NORM)
        for i in range(KEY_TILES):
            qkg_out[2 * KEY_TILES + i] = gk[:, i * LANES:(i + 1) * LANES]

    n_cols = 2 * GLA_KEY + 2 * GLA_VAL + POOL_WIDTH
    return [norm] + [piece(c) for c in range(0, n_cols, PROJ_COLS)] + [gate]


def _mix_kernel(x_next_ref, x_ref, ln_ref, wmain_ref, wglr_ref, wgate_ref, bgate_ref,
                tril_ref, esel_ref, gnorm_ref, wpool_ref, pscale_ref, wout_ref, h_ref,
                qkg_ref, vg_ref, u_ref, qkg_next_ref, vg_next_ref, u_next_ref, state_ref,
                ubuf_ref, b_ref, w_ref, qoff_ref, o_ref, wcomb_ref, *, tm, tiles_per_seq):
    n = pl.program_id(0)
    t = n % tiles_per_seq
    n_chunks = tm // GLA_CHUNK
    weights = (ln_ref, wmain_ref, wglr_ref, wgate_ref, bgate_ref)

    @pl.when(n == 0)
    def _():
        for step in _projection_steps(x_ref, *weights, qkg_ref, vg_ref, u_ref):
            step()
        for g in range(len(POOL_WINDOWS)):
            rows = slice(g * POOL_GC, (g + 1) * POOL_GC)
            scaled = (wpool_ref[g] * pscale_ref[:, rows]).astype(BF16)
            wcomb_ref[rows, :] = _dot(
                scaled, wout_ref[GLA_VAL + g * POOL_GC:GLA_VAL + (g + 1) * POOL_GC, :]).astype(BF16)

    @pl.when(t == 0)
    def _():
        state_ref[...] = jnp.zeros_like(state_ref)
        ubuf_ref[0:POOL_HALO, :] = jnp.zeros((POOL_HALO, POOL_WIDTH), F32)

    pending = _projection_steps(x_next_ref, *weights, qkg_next_ref, vg_next_ref, u_next_ref)

    def project_next(k):
        for _ in range(min(k, len(pending))):
            pending.pop(0)()

    vg = vg_ref
    project_next(1)

    def q_tile(i):
        return qkg_ref.at[i]

    def k_tile(i):
        return qkg_ref.at[KEY_TILES + i]

    gk = jnp.concatenate([qkg_ref[2 * KEY_TILES + i] for i in range(KEY_TILES)], axis=1)
    g_hi = gk.astype(BF16)
    rem = gk - g_hi.astype(F32)
    g_mid = rem.astype(BF16)
    g_lo = (rem - g_mid.astype(F32)).astype(BF16)
    tril = tril_ref[...]
    for r0 in range(0, tm, TRIL_ROWS):
        rs = slice(r0, r0 + TRIL_ROWS)
        b_blk = _dot(tril, g_hi[rs]) + _dot(tril, g_mid[rs]) + _dot(tril, g_lo[rs])
        for i in range(KEY_TILES):
            b_ref[i, rs, :] = b_blk[:, i * LANES:(i + 1) * LANES]

    row8 = lax.broadcasted_iota(jnp.int32, (SUBLANES, LANES), 0)
    neg_inf = jnp.full((SUBLANES, LANES), -jnp.inf, F32)
    zeros8 = jnp.zeros((SUBLANES, LANES), F32)

    def group_body(g):
        r0 = g * SUB_BLOCK
        for i in range(KEY_TILES):
            b_lo = b_ref[i, pl.ds(r0, SUBLANES), :]
            b_hi = b_ref[i, pl.ds(r0 + SUBLANES, SUBLANES), :]
            q_lo = q_tile(i)[pl.ds(r0, SUBLANES), :]
            q_hi = q_tile(i)[pl.ds(r0 + SUBLANES, SUBLANES), :]
            k_lo = k_tile(i)[pl.ds(r0, SUBLANES), :]
            k_hi = k_tile(i)[pl.ds(r0 + SUBLANES, SUBLANES), :]
            for j in range(SUB_BLOCK):
                s = j % SUBLANES
                b_j = _row_bcast(b_lo if j < SUBLANES else b_hi, s, SUBLANES)
                k_j = _row_bcast(k_lo if j < SUBLANES else k_hi, s, SUBLANES)
                if j < SUBLANES:
                    e_lo = jnp.exp2(jnp.where(row8 >= j, b_lo - b_j, neg_inf))
                    e_hi = jnp.exp2(b_hi - b_j)
                    w_lo = q_lo * k_j * e_lo
                else:
                    e_hi = jnp.exp2(jnp.where(row8 >= j - SUBLANES, b_hi - b_j, neg_inf))
                    w_lo = zeros8
                w_hi = q_hi * k_j * e_hi
                col = j * GLA_KEY + i * LANES
                w_ref[pl.ds(r0, SUB_BLOCK), col:col + LANES] = (
                    jnp.concatenate([w_lo, w_hi], axis=0).astype(BF16))
                if j == 0:
                    qoff_ref[pl.ds(r0, SUB_BLOCK), i * LANES:(i + 1) * LANES] = (
                        jnp.concatenate([q_lo * e_lo, q_hi * e_hi], axis=0))
    a_parts = []
    for r0 in range(0, tm, DIAG_ROWS):
        for g in range(r0 // SUB_BLOCK, (r0 + DIAG_ROWS) // SUB_BLOCK):
            group_body(g)
            if g in PROJECT_AFTER_GROUP:
                project_next(1)
        a_parts.append(_dot(w_ref[r0:r0 + DIAG_ROWS, :], esel_ref[...]))
    a_diag = jnp.concatenate(a_parts, axis=0)

    lane_head = (lax.broadcasted_iota(jnp.int32, (tm, GLA_KEY), 1) % LANES) // GLA_DK
    blk_row = (lax.broadcasted_iota(jnp.int32, (tm, GLA_KEY), 0) % GLA_CHUNK) // SUB_BLOCK
    blk_lane = (lax.broadcasted_iota(jnp.int32, (tm, GLA_KEY), 1) % GLA_CHUNK) // SUB_BLOCK
    b_all = jnp.concatenate([b_ref[i] for i in range(KEY_TILES)], axis=1)
    q_all = jnp.concatenate([q_tile(i)[...] for i in range(KEY_TILES)], axis=1)
    q_in = (q_all * jnp.exp2(b_all)).astype(BF16)
    a_dg = jnp.where(blk_row == blk_lane, a_diag, 0.0).astype(BF16)
    q_off_all = qoff_ref[...]
    q_off = [jnp.where(lane_head == hh, q_off_all, 0.0).astype(BF16)
             for hh in range(HEADS_PER_TILE)]

    stacked = HEADS_PER_TILE * GLA_CHUNK
    prow = (lax.broadcasted_iota(jnp.int32, (stacked, N_CAT), 0) % GLA_CHUNK) // SUB_BLOCK
    pcol = lax.broadcasted_iota(jnp.int32, (stacked, N_CAT), 1)
    pgrp = jnp.zeros_like(pcol)
    start = 0
    for i in range(1, N_SUB):
        pgrp = jnp.where((pcol >= start) & (pcol < start + i * SUB_BLOCK), i, pgrp)
        start += i * SUB_BLOCK
    off_mask = prow == pgrp
    first_head_lanes = lax.broadcasted_iota(jnp.int32, (GLA_DV, LANES), 1) < GLA_DK
    first_head_rows = lax.broadcasted_iota(jnp.int32, (LANES, GLA_DV), 0) < GLA_DK
    zeros_v = jnp.zeros((GLA_CHUNK, GLA_DV), BF16)

    p_off = {}
    d_state = {}
    decay = {}
    for c in range(n_chunks):
        r0 = c * GLA_CHUNK
        rs = slice(r0, r0 + GLA_CHUNK)
        for i in range(KEY_TILES):
            ls = slice(i * LANES, (i + 1) * LANES)
            bc = b_ref[i, rs, :]
            kc = k_tile(i)[rs, :]
            b_last = _row_bcast(bc, GLA_CHUNK - 1, GLA_CHUNK)
            k_dec = (kc * jnp.exp2(b_last - bc)).astype(BF16)
            decay[c, i] = jnp.exp2(b_last[0:SUBLANES])
            pieces = []
            for s in range(1, N_SUB):
                m = s * SUB_BLOCK
                r_s = _row_bcast(bc, m, m)
                pieces.append(kc[0:m] * jnp.exp2(r_s - bc[0:m]))
            k_cat = jnp.concatenate(pieces, axis=0).astype(BF16)
            q_two = jnp.concatenate([q[rs, ls] for q in q_off], axis=0)
            p_off[c, i] = jnp.where(off_mask, _dot(q_two, k_cat, NT), 0.0).astype(BF16)
            ds = _dot(vg[rs, i * HEADS_PER_TILE * GLA_DV:(i + 1) * HEADS_PER_TILE * GLA_DV],
                      k_dec, TN)
            d_state[c, i] = jnp.where(first_head_lanes, ds[0:GLA_DV], ds[GLA_DV:])
        if c in PROJECT_AFTER_SCORES:
            project_next(1)

    state_in = {}
    for i in range(KEY_TILES):
        ls = slice(i * LANES, (i + 1) * LANES)
        s = state_ref[:, ls]
        for c in range(n_chunks):
            s_t = s.T
            state_in[c, i] = jnp.concatenate(
                [jnp.where(first_head_rows, s_t, 0.0), jnp.where(first_head_rows, 0.0, s_t)],
                axis=1).astype(BF16)
            s = (s.reshape(GLA_DV // SUBLANES, SUBLANES, LANES) * decay[c, i][None]
                 ).reshape(GLA_DV, LANES) + d_state[c, i]
        state_ref[:, ls] = s

    for c in range(n_chunks):
        r0 = c * GLA_CHUNK
        rs = slice(r0, r0 + GLA_CHUNK)
        for i in range(KEY_TILES):
            ls = slice(i * LANES, (i + 1) * LANES)
            vs = slice(i * HEADS_PER_TILE * GLA_DV, (i + 1) * HEADS_PER_TILE * GLA_DV)
            v_heads = [vg[rs, (i * HEADS_PER_TILE + hh) * GLA_DV:(i * HEADS_PER_TILE + hh + 1) * GLA_DV]
                       for hh in range(HEADS_PER_TILE)]
            off = []
            for hh, vc in enumerate(v_heads):
                v_cat = jnp.concatenate([vc[0:s * SUB_BLOCK] for s in range(1, N_SUB)], axis=0)
                off.append(_dot(p_off[c, i][hh * GLA_CHUNK:(hh + 1) * GLA_CHUNK], v_cat))
            v_diag = jnp.concatenate(
                [jnp.concatenate([v_heads[0], zeros_v], axis=1),
                 jnp.concatenate([zeros_v, v_heads[1]], axis=1)], axis=0)
            o_ref[rs, vs] = (jnp.concatenate(off, axis=1) + _dot(a_dg[rs, ls], v_diag)
                             + _dot(q_in[rs, ls], state_in[c, i]))
        if c in PROJECT_AFTER_OUTPUTS:
            project_next(1)

    gated = []
    for h in range(GLA_HEADS):
        vs = slice(h * GLA_DV, (h + 1) * GLA_DV)
        o_h = _rmsnorm(o_ref[:, vs], gnorm_ref[...])
        g_h = vg[:, GLA_VAL + h * GLA_DV:GLA_VAL + (h + 1) * GLA_DV].astype(F32)
        gated.append((o_h * (g_h * jax.nn.sigmoid(g_h))).astype(BF16))
    project_next(len(pending))
    mix = _dot(jnp.concatenate(gated, axis=1), wout_ref[0:GLA_VAL, :])

    u = u_ref[...]
    ubuf_ref[POOL_HALO:POOL_HALO + tm, :] = u
    pos = t * tm + lax.broadcasted_iota(jnp.int32, (tm, 1), 0)
    pooled = []
    for g, w in enumerate(POOL_WINDOWS):
        ls = slice(g * POOL_GC, (g + 1) * POOL_GC)
        acc = ubuf_ref[:, ls]
        lo, step = 0, 1
        while step < w:
            acc = acc[step:] + acc[:-step]
            lo += step
            step *= 2
        acc = acc[POOL_HALO - lo:]
        cnt = jnp.minimum(pos + 1, w).astype(F32)
        pooled.append((acc / cnt - u[:, ls]).astype(BF16))
    ubuf_ref[0:POOL_HALO, :] = ubuf_ref[tm:tm + POOL_HALO, :]
    mix = mix + _dot(jnp.concatenate(pooled, axis=1), wcomb_ref[...])

    h_ref[...] = x_ref[...] + mix

    qkg_ref[...] = qkg_next_ref[...]
    vg_ref[...] = vg_next_ref[...]
    u_ref[...] = u_next_ref[...]


def _ffn_kernel(h_ref, p_ref, lnf_ref, wg_ref, wu_ref, wd_ref, lnp_ref, wpg_ref, wpp_ref,
                lnfin_ref, out_ref, hid_ref):
    ple = _dot(p_ref[...].astype(BF16), wpp_ref[...])
    h = h_ref[...]
    f = _rmsnorm(h, lnf_ref[...]).astype(BF16)
    for c in range(0, D_FF, FF_CHUNK):
        g = _dot(f, wg_ref[:, c:c + FF_CHUNK])
        u = _dot(f, wu_ref[:, c:c + FF_CHUNK])
        hid_ref[:, c:c + FF_CHUNK] = (g * jax.nn.sigmoid(g) * u).astype(BF16)
    h = h + _dot(hid_ref[...], wd_ref[...])
    e = _rmsnorm(h, lnp_ref[...]).astype(BF16)
    gate = jax.nn.sigmoid(_dot(e, wpg_ref[...]))
    h = h + gate * ple
    out_ref[...] = _rmsnorm(h, lnfin_ref[...])


def _selection_matrix():
    r = np.arange(SUB_BLOCK * GLA_KEY)
    j, hd = r // GLA_KEY, r % GLA_KEY
    c = np.arange(GLA_KEY)
    same_head = (hd // GLA_DK)[:, None] == (c // GLA_CHUNK)[None, :]
    same_col = j[:, None] == (c % SUB_BLOCK)[None, :]
    return jnp.asarray(same_head & same_col, dtype=BF16)


def _chunk_tril():
    r = np.arange(TRIL_ROWS)
    same_chunk = (r // GLA_CHUNK)[:, None] == (r // GLA_CHUNK)[None, :]
    return jnp.asarray(same_chunk & (r[None, :] <= r[:, None]), dtype=BF16)


@jax.jit
def _forward(x, p, ln_mix, w_in, w_gate_lr, b_gate, gla_norm, w_pool, pool_scale, w_out,
             ln_ffn, w_ffn_gate, w_ffn_up, w_ffn_down, ln_ple, w_ple_gate, w_ple_proj,
             ln_final):
    bsz, seq, _ = x.shape
    n_tok = bsz * seq
    h = x.reshape(n_tok, D_MODEL)
    assert ln_mix.shape[0] == 1
    s_glr = 2 * GLA_KEY + 2 * GLA_VAL
    w_main = jnp.concatenate(
        [w_in[0][:, :s_glr], w_in[0][:, s_glr + GLA_GATE_RANK:]], axis=1).astype(BF16)
    w_glr = jnp.pad(w_in[0][:, s_glr:s_glr + GLA_GATE_RANK],
                    ((0, 0), (0, LANES - GLA_GATE_RANK))).astype(BF16)
    w_gate = jnp.pad(w_gate_lr[0], ((0, LANES - GLA_GATE_RANK), (0, 0))).astype(BF16)

    tm = TM_MIX
    n_tiles = n_tok // tm
    h = pl.pallas_call(
        functools.partial(_mix_kernel, tm=tm, tiles_per_seq=seq // tm),
        grid=(n_tiles,),
        in_specs=[
            pl.BlockSpec((tm, D_MODEL), lambda n: (jnp.minimum(n + 1, n_tiles - 1), 0)),
            pl.BlockSpec((tm, D_MODEL), lambda n: (n, 0)),
            _const_spec((1, D_MODEL)),
            _const_spec((D_MODEL, 2 * GLA_KEY + 2 * GLA_VAL + POOL_WIDTH)),
            _const_spec((D_MODEL, LANES)),
            _const_spec((LANES, GLA_KEY)),
            _const_spec((1, GLA_KEY)),
            _const_spec((TRIL_ROWS, TRIL_ROWS)),
            _const_spec((SUB_BLOCK * GLA_KEY, GLA_KEY)),
            _const_spec((1, GLA_DV)),
            _const_spec((len(POOL_WINDOWS), POOL_GC, POOL_GC)),
            _const_spec((1, POOL_WIDTH)),
            _const_spec((D_MODEL, D_MODEL)),
        ],
        out_specs=pl.BlockSpec((tm, D_MODEL), lambda n: (n, 0)),
        out_shape=jax.ShapeDtypeStruct((n_tok, D_MODEL), F32),
        scratch_shapes=[
            pltpu.VMEM((3 * KEY_TILES, tm, LANES), F32),
            pltpu.VMEM((tm, 2 * GLA_VAL), BF16),
            pltpu.VMEM((tm, POOL_WIDTH), F32),
            pltpu.VMEM((3 * KEY_TILES, tm, LANES), F32),
            pltpu.VMEM((tm, 2 * GLA_VAL), BF16),
            pltpu.VMEM((tm, POOL_WIDTH), F32),
            pltpu.VMEM((GLA_DV, GLA_KEY), F32),
            pltpu.VMEM((POOL_HALO + tm, POOL_WIDTH), F32),
            pltpu.VMEM((KEY_TILES, tm, LANES), F32),
            pltpu.VMEM((tm, SUB_BLOCK * GLA_KEY), BF16),
            pltpu.VMEM((tm, GLA_KEY), F32),
            pltpu.VMEM((tm, GLA_VAL), F32),
            pltpu.VMEM((POOL_WIDTH, D_MODEL), BF16),
        ],
        compiler_params=pltpu.CompilerParams(
            dimension_semantics=("arbitrary",), vmem_limit_bytes=VMEM_LIMIT),
        name="mix",
    )(h, h, ln_mix[0][None], w_main, w_glr, w_gate, b_gate[0][None], _chunk_tril(),
      _selection_matrix(), gla_norm[0][None], w_pool[0], pool_scale[0][None],
      w_out[0].astype(BF16))

    h = pl.pallas_call(
        _ffn_kernel,
        grid=(n_tok // TM_FFN,),
        in_specs=[
            pl.BlockSpec((TM_FFN, D_MODEL), lambda r: (r, 0)),
            pl.BlockSpec((TM_FFN, D_PLE), lambda r: (r, 0)),
            _const_spec((1, D_MODEL)),
            _const_spec((D_MODEL, D_FF)),
            _const_spec((D_MODEL, D_FF)),
            _const_spec((D_FF, D_MODEL)),
            _const_spec((1, D_MODEL)),
            _const_spec((D_MODEL, D_MODEL)),
            _const_spec((D_PLE, D_MODEL)),
            _const_spec((1, D_MODEL)),
        ],
        out_specs=pl.BlockSpec((TM_FFN, D_MODEL), lambda r: (r, 0)),
        out_shape=jax.ShapeDtypeStruct((n_tok, D_MODEL), F32),
        scratch_shapes=[pltpu.VMEM((TM_FFN, D_FF), BF16)],
        compiler_params=pltpu.CompilerParams(
            dimension_semantics=("arbitrary",), vmem_limit_bytes=VMEM_LIMIT),
        name="ffn",
    )(h, p[0].reshape(n_tok, D_PLE), ln_ffn[0][None], w_ffn_gate[0].astype(BF16),
      w_ffn_up[0].astype(BF16), w_ffn_down[0].astype(BF16), ln_ple[0][None],
      w_ple_gate[0].astype(BF16), w_ple_proj[0].astype(BF16), ln_final[None])
    return h.reshape(bsz, seq, D_MODEL)


def kernel(x, p, ln_mix, w_in, w_gate_lr, b_gate, gla_norm, w_pool, pool_scale, w_out, ln_ffn,
           w_ffn_gate, w_ffn_up, w_ffn_down, ln_ple, w_ple_gate, w_ple_proj, ln_final):
    return _forward(x, p, ln_mix, w_in, w_gate_lr, b_gate, gla_norm, w_pool, pool_scale, w_out,
                    ln_ffn, w_ffn_gate, w_ffn_up, w_ffn_down, ln_ple, w_ple_gate, w_ple_proj,
                    ln_final)
```

```python
import functools
import math

import jax
import jax.numpy as jnp
import numpy as np
from jax import lax
from jax.experimental import pallas as pl
from jax.experimental.pallas import tpu as pltpu

D_MODEL = 1024
D_PLE = 256
GLA_HEADS = 4
GLA_DK = 64
GLA_DV = 128
GLA_KEY = GLA_HEADS * GLA_DK
GLA_VAL = GLA_HEADS * GLA_DV
GLA_GATE_RANK = 16
GLA_GATE_NORM = 16.0
GLA_CHUNK = 64
POOL_WINDOWS = (2, 4, 8, 16)
POOL_GC = 128
POOL_WIDTH = 512
D_FF = 2816
EPS = 1e-6

LANES = 128
SUBLANES = 8
KEY_TILES = GLA_KEY // LANES
HEADS_PER_TILE = LANES // GLA_DK
SUB_BLOCK = 16
N_SUB = GLA_CHUNK // SUB_BLOCK
N_CAT = SUB_BLOCK * (N_SUB * (N_SUB - 1) // 2)
TRIL_ROWS = 128
PROJ_COLS = 256
PROJECT_AFTER_PAIR = (1, 4, 7, 11)
PROJECT_AFTER_SCORES = (3, 7)
PROJECT_AFTER_OUTPUTS = (3,)
POOL_HALO = max(POOL_WINDOWS)
FF_CHUNK = 256
TM_MIX = 512
TM_FFN = 1024
VMEM_LIMIT = 56 * 1024 * 1024
LOG2E = math.log2(math.e)

F32 = jnp.float32
BF16 = jnp.bfloat16
NT = (((1,), (1,)), ((), ()))
TN = (((0,), (0,)), ((), ()))


def _dot(a, b, dims=None):
    if dims is None:
        return jnp.dot(a, b, preferred_element_type=F32)
    return lax.dot_general(a, b, dims, preferred_element_type=F32)


def _rmsnorm(x, g):
    return x * lax.rsqrt(jnp.mean(x * x, axis=-1, keepdims=True) + EPS) * g


def _row_bcast(x, row, n):
    return jnp.broadcast_to(x[row:row + 1, :], (n, x.shape[1]))


def _const_spec(shape):
    nd = len(shape)
    return pl.BlockSpec(shape, lambda *_: (0,) * nd, pipeline_mode=pl.Buffered(1))


def _projection_steps(x_ref, ln_ref, wmain_ref, wglr_ref, wgate_ref, bgate_ref,
                      qkg_out, vg_out, u_out):
    hold = {}

    def norm():
        hold["a"] = _rmsnorm(x_ref[...], ln_ref[...]).astype(BF16)

    def piece(col):
        def run():
            z = _dot(hold["a"], wmain_ref[:, col:col + PROJ_COLS])
            for i in range(PROJ_COLS // LANES):
                zi = z[:, i * LANES:(i + 1) * LANES]
                c = col + i * LANES
                if c < GLA_KEY:
                    qkg_out[c // LANES] = zi * (GLA_DK ** -0.5)
                elif c < 2 * GLA_KEY:
                    qkg_out[c // LANES] = zi
                elif c < 2 * GLA_KEY + 2 * GLA_VAL:
                    vg_out[:, c - 2 * GLA_KEY:c - 2 * GLA_KEY + LANES] = zi.astype(BF16)
                else:
                    c -= 2 * GLA_KEY + 2 * GLA_VAL
                    u_out[:, c:c + LANES] = zi
        return run

    def gate():
        g_lr = _dot(hold["a"], wglr_ref[...])
        pre = _dot(g_lr.astype(BF16), wgate_ref[...]) + bgate_ref[...]
        gk = (jnp.minimum(pre, 0.0) - jnp.log1p(jnp.exp(-jnp.abs(pre)))) * (LOG2E / GLA_GATE_NORM)
        for i in range(KEY_TILES):
            qkg_out[2 * KEY_TILES + i] = gk[:, i * LANES:(i + 1) * LANES]

    n_cols = 2 * GLA_KEY + 2 * GLA_VAL + POOL_WIDTH
    return [norm] + [piece(c) for c in range(0, n_cols, PROJ_COLS)] + [gate]


def _mix_kernel(x_next_ref, x_ref, ln_ref, win_ref, wgatelr_ref, bgate_ref,
                tril_ref, esel_ref, gnorm_ref, wpool_ref, pscale_ref, wout32_ref, h_ref,
                wmain_ref, wglr_ref, wgate_ref, wout_ref, wcomb_ref,
                qkg_ref, vg_ref, u_ref, qkg_next_ref, vg_next_ref, u_next_ref, state_ref,
                ubuf_ref, b_ref, w1_ref, w2_ref, qoff_ref, o_ref, *, tm, tiles_per_seq):
    n = pl.program_id(0)
    t = n % tiles_per_seq
    n_chunks = tm // GLA_CHUNK
    weights = (ln_ref, wmain_ref, wglr_ref, wgate_ref, bgate_ref)

    @pl.when(n == 0)
    def _():
        s_glr = 2 * GLA_KEY + 2 * GLA_VAL
        wmain_ref[:, 0:s_glr] = win_ref[:, 0:s_glr].astype(BF16)
        wmain_ref[:, s_glr:] = win_ref[:, s_glr + GLA_GATE_RANK:].astype(BF16)
        wglr_ref[...] = jnp.zeros_like(wglr_ref)
        wglr_ref[:, 0:GLA_GATE_RANK] = win_ref[:, s_glr:s_glr + GLA_GATE_RANK].astype(BF16)
        wgate_ref[...] = jnp.zeros_like(wgate_ref)
        wgate_ref[0:GLA_GATE_RANK, :] = wgatelr_ref[...].astype(BF16)
        wout_ref[...] = wout32_ref[...].astype(BF16)
        for step in _projection_steps(x_ref, *weights, qkg_ref, vg_ref, u_ref):
            step()
        for g in range(len(POOL_WINDOWS)):
            rows = slice(g * POOL_GC, (g + 1) * POOL_GC)
            scaled = (wpool_ref[g] * pscale_ref[:, rows]).astype(BF16)
            wcomb_ref[rows, :] = _dot(
                scaled, wout_ref[GLA_VAL + g * POOL_GC:GLA_VAL + (g + 1) * POOL_GC, :]).astype(BF16)

    @pl.when(t == 0)
    def _():
        state_ref[...] = jnp.zeros_like(state_ref)
        ubuf_ref[0:POOL_HALO, :] = jnp.zeros((POOL_HALO, POOL_WIDTH), F32)

    pending = _projection_steps(x_next_ref, *weights, qkg_next_ref, vg_next_ref, u_next_ref)

    def project_next(k):
        for _ in range(min(k, len(pending))):
            pending.pop(0)()

    vg = vg_ref
    project_next(1)

    def q_tile(i):
        return qkg_ref.at[i]

    def k_tile(i):
        return qkg_ref.at[KEY_TILES + i]

    gk = jnp.concatenate([qkg_ref[2 * KEY_TILES + i] for i in range(KEY_TILES)], axis=1)
    g_hi = gk.astype(BF16)
    rem = gk - g_hi.astype(F32)
    g_mid = rem.astype(BF16)
    g_lo = (rem - g_mid.astype(F32)).astype(BF16)
    tril = tril_ref[...]
    for r0 in range(0, tm, TRIL_ROWS):
        rs = slice(r0, r0 + TRIL_ROWS)
        b_blk = _dot(tril, g_hi[rs]) + _dot(tril, g_mid[rs]) + _dot(tril, g_lo[rs])
        for i in range(KEY_TILES):
            b_ref[i, rs, :] = b_blk[:, i * LANES:(i + 1) * LANES]

    row8 = lax.broadcasted_iota(jnp.int32, (SUBLANES, LANES), 0)
    neg_inf = jnp.full((SUBLANES, LANES), -jnp.inf, F32)

    def pair_body(p, first_pass):
        half = tm // 2
        for i in range(KEY_TILES):
            ld = lambda ref, g, hi: ref[pl.ds(g * SUB_BLOCK + hi * SUBLANES, SUBLANES), :]
            groups = (2 * p, 2 * p + 1)
            src = 0 if first_pass else 1
            b_hi = [ld(b_ref.at[i], g, 1) for g in groups]
            q_hi = [ld(q_tile(i), g, 1) for g in groups]
            b_src = [ld(b_ref.at[i], g, src) for g in groups]
            k_src = [ld(k_tile(i), g, src) for g in groups]
            if first_pass:
                q_lo = [ld(q_tile(i), g, 0) for g in groups]
            for s in range(SUBLANES):
                w_lo, w_hi = [], []
                for n_g, g in enumerate(groups):
                    b_j = _row_bcast(b_src[n_g], s, SUBLANES)
                    k_j = _row_bcast(k_src[n_g], s, SUBLANES)
                    if first_pass:
                        e_lo = jnp.exp2(jnp.where(row8 >= s, b_src[n_g] - b_j, neg_inf))
                        e_hi = jnp.exp2(b_hi[n_g] - b_j)
                        w_lo.append(q_lo[n_g] * k_j * e_lo)
                    else:
                        e_hi = jnp.exp2(jnp.where(row8 >= s, b_hi[n_g] - b_j, neg_inf))
                    w_hi.append(q_hi[n_g] * k_j * e_hi)
                    if first_pass and s == 0:
                        qoff_ref[pl.ds(g * SUB_BLOCK, SUB_BLOCK), i * LANES:(i + 1) * LANES] = (
                            jnp.concatenate([q_lo[n_g] * e_lo, q_hi[n_g] * e_hi], axis=0))
                col = s * GLA_KEY + i * LANES
                w_hi = jnp.concatenate(w_hi, axis=0).astype(BF16)
                if first_pass:
                    w1_ref[pl.ds(p * SUB_BLOCK, SUB_BLOCK), col:col + LANES] = (
                        jnp.concatenate(w_lo, axis=0).astype(BF16))
                    w1_ref[pl.ds(half + p * SUB_BLOCK, SUB_BLOCK), col:col + LANES] = w_hi
                else:
                    w2_ref[pl.ds(p * SUB_BLOCK, SUB_BLOCK), col:col + LANES] = w_hi

    n_grp = tm // SUB_BLOCK
    half_k = SUBLANES * GLA_KEY
    for p in range(n_grp // 2):
        pair_body(p, True)
        if p in PROJECT_AFTER_PAIR:
            project_next(1)
    a_first = _dot(w1_ref[...], esel_ref[0:half_k, :])
    for p in range(n_grp // 2):
        pair_body(p, False)
    a_lo = a_first[0:tm // 2]
    a_hi = a_first[tm // 2:] + _dot(w2_ref[...], esel_ref[half_k:, :])
    a_diag = jnp.concatenate(
        [a_lo.reshape(n_grp, SUBLANES, GLA_KEY), a_hi.reshape(n_grp, SUBLANES, GLA_KEY)],
        axis=1).reshape(tm, GLA_KEY)

    lane_head = (lax.broadcasted_iota(jnp.int32, (tm, GLA_KEY), 1) % LANES) // GLA_DK
    blk_row = (lax.broadcasted_iota(jnp.int32, (tm, GLA_KEY), 0) % GLA_CHUNK) // SUB_BLOCK
    blk_lane = (lax.broadcasted_iota(jnp.int32, (tm, GLA_KEY), 1) % GLA_CHUNK) // SUB_BLOCK
    b_all = jnp.concatenate([b_ref[i] for i in range(KEY_TILES)], axis=1)
    q_all = jnp.concatenate([q_tile(i)[...] for i in range(KEY_TILES)], axis=1)
    q_in = (q_all * jnp.exp2(b_all)).astype(BF16)
    a_dg = jnp.where(blk_row == blk_lane, a_diag, 0.0).astype(BF16)
    q_off_all = qoff_ref[...]
    q_off = [jnp.where(lane_head == hh, q_off_all, 0.0).astype(BF16)
             for hh in range(HEADS_PER_TILE)]

    stacked = HEADS_PER_TILE * GLA_CHUNK
    prow = (lax.broadcasted_iota(jnp.int32, (stacked, N_CAT), 0) % GLA_CHUNK) // SUB_BLOCK
    pcol = lax.broadcasted_iota(jnp.int32, (stacked, N_CAT), 1)
    pgrp = jnp.zeros_like(pcol)
    start = 0
    for i in range(1, N_SUB):
        pgrp = jnp.where((pcol >= start) & (pcol < start + i * SUB_BLOCK), i, pgrp)
        start += i * SUB_BLOCK
    off_mask = prow == pgrp
    first_head_lanes = lax.broadcasted_iota(jnp.int32, (GLA_DV, LANES), 1) < GLA_DK
    first_head_rows = lax.broadcasted_iota(jnp.int32, (LANES, GLA_DV), 0) < GLA_DK
    zeros_v = jnp.zeros((GLA_CHUNK, GLA_DV), BF16)

    p_off = {}
    d_state = {}
    decay = {}
    for c in range(n_chunks):
        r0 = c * GLA_CHUNK
        rs = slice(r0, r0 + GLA_CHUNK)
        for i in range(KEY_TILES):
            ls = slice(i * LANES, (i + 1) * LANES)
            bc = b_ref[i, rs, :]
            kc = k_tile(i)[rs, :]
            b_last = _row_bcast(bc, GLA_CHUNK - 1, GLA_CHUNK)
            k_dec = (kc * jnp.exp2(b_last - bc)).astype(BF16)
            decay[c, i] = jnp.exp2(b_last[0:SUBLANES])
            pieces = []
            for s in range(1, N_SUB):
                m = s * SUB_BLOCK
                r_s = _row_bcast(bc, m, m)
                pieces.append(kc[0:m] * jnp.exp2(r_s - bc[0:m]))
            k_cat = jnp.concatenate(pieces, axis=0).astype(BF16)
            q_two = jnp.concatenate([q[rs, ls] for q in q_off], axis=0)
            p_off[c, i] = jnp.where(off_mask, _dot(q_two, k_cat, NT), 0.0).astype(BF16)
            ds = _dot(vg[rs, i * HEADS_PER_TILE * GLA_DV:(i + 1) * HEADS_PER_TILE * GLA_DV],
                      k_dec, TN)
            d_state[c, i] = jnp.where(first_head_lanes, ds[0:GLA_DV], ds[GLA_DV:])
        if c in PROJECT_AFTER_SCORES:
            project_next(1)

    state_in = {}
    for i in range(KEY_TILES):
        ls = slice(i * LANES, (i + 1) * LANES)
        s = state_ref[:, ls]
        for c in range(n_chunks):
            s_t = s.T
            state_in[c, i] = jnp.concatenate(
                [jnp.where(first_head_rows, s_t, 0.0), jnp.where(first_head_rows, 0.0, s_t)],
                axis=1).astype(BF16)
            s = (s.reshape(GLA_DV // SUBLANES, SUBLANES, LANES) * decay[c, i][None]
                 ).reshape(GLA_DV, LANES) + d_state[c, i]
        state_ref[:, ls] = s

    for c in range(n_chunks):
        r0 = c * GLA_CHUNK
        rs = slice(r0, r0 + GLA_CHUNK)
        for i in range(KEY_TILES):
            ls = slice(i * LANES, (i + 1) * LANES)
            vs = slice(i * HEADS_PER_TILE * GLA_DV, (i + 1) * HEADS_PER_TILE * GLA_DV)
            v_heads = [vg[rs, (i * HEADS_PER_TILE + hh) * GLA_DV:(i * HEADS_PER_TILE + hh + 1) * GLA_DV]
                       for hh in range(HEADS_PER_TILE)]
            off = []
            for hh, vc in enumerate(v_heads):
                v_cat = jnp.concatenate([vc[0:s * SUB_BLOCK] for s in range(1, N_SUB)], axis=0)
                off.append(_dot(p_off[c, i][hh * GLA_CHUNK:(hh + 1) * GLA_CHUNK], v_cat))
            v_diag = jnp.concatenate(
                [jnp.concatenate([v_heads[0], zeros_v], axis=1),
                 jnp.concatenate([zeros_v, v_heads[1]], axis=1)], axis=0)
            o_ref[rs, vs] = (jnp.concatenate(off, axis=1) + _dot(a_dg[rs, ls], v_diag)
                             + _dot(q_in[rs, ls], state_in[c, i]))
        if c in PROJECT_AFTER_OUTPUTS:
            project_next(1)

    gated = []
    for h in range(GLA_HEADS):
        vs = slice(h * GLA_DV, (h + 1) * GLA_DV)
        o_h = _rmsnorm(o_ref[:, vs], gnorm_ref[...])
        g_h = vg[:, GLA_VAL + h * GLA_DV:GLA_VAL + (h + 1) * GLA_DV].astype(F32)
        gated.append((o_h * (g_h * jax.nn.sigmoid(g_h))).astype(BF16))
    project_next(len(pending))
    mix = _dot(jnp.concatenate(gated, axis=1), wout_ref[0:GLA_VAL, :])

    u = u_ref[...]
    ubuf_ref[POOL_HALO:POOL_HALO + tm, :] = u
    pos = t * tm + lax.broadcasted_iota(jnp.int32, (tm, 1), 0)
    pooled = []
    for g, w in enumerate(POOL_WINDOWS):
        ls = slice(g * POOL_GC, (g + 1) * POOL_GC)
        acc = ubuf_ref[:, ls]
        lo, step = 0, 1
        while step < w:
            acc = acc[step:] + acc[:-step]
            lo += step
            step *= 2
        acc = acc[POOL_HALO - lo:]
        cnt = jnp.minimum(pos + 1, w).astype(F32)
        pooled.append((acc / cnt - u[:, ls]).astype(BF16))
    ubuf_ref[0:POOL_HALO, :] = ubuf_ref[tm:tm + POOL_HALO, :]
    mix = mix + _dot(jnp.concatenate(pooled, axis=1), wcomb_ref[...])

    h_ref[...] = x_ref[...] + mix

    qkg_ref[...] = qkg_next_ref[...]
    vg_ref[...] = vg_next_ref[...]
    u_ref[...] = u_next_ref[...]


def _ffn_kernel(h_ref, p_ref, lnf_ref, wg_ref, wu_ref, wd_ref, lnp_ref, wpg_ref, wpp_ref,
                lnfin_ref, out_ref, hid_ref):
    ple = _dot(p_ref[...].astype(BF16), wpp_ref[...])
    h = h_ref[...]
    f = _rmsnorm(h, lnf_ref[...]).astype(BF16)
    for c in range(0, D_FF, FF_CHUNK):
        g = _dot(f, wg_ref[:, c:c + FF_CHUNK])
        u = _dot(f, wu_ref[:, c:c + FF_CHUNK])
        hid_ref[:, c:c + FF_CHUNK] = (g * jax.nn.sigmoid(g) * u).astype(BF16)
    half = h.shape[0] // 2
    halves = (slice(0, half), slice(half, 2 * half))
    mid = [h[rows] + _dot(hid_ref[rows, :], wd_ref[...]) for rows in halves]
    gates = [jax.nn.sigmoid(_dot(_rmsnorm(x, lnp_ref[...]).astype(BF16), wpg_ref[...]))
             for x in mid]
    for rows, x, gate in zip(halves, mid, gates):
        out_ref[rows, :] = _rmsnorm(x + gate * ple[rows], lnfin_ref[...])


def _selection_matrix():
    r = np.arange(SUB_BLOCK * GLA_KEY)
    j, hd = r // GLA_KEY, r % GLA_KEY
    c = np.arange(GLA_KEY)
    same_head = (hd // GLA_DK)[:, None] == (c // GLA_CHUNK)[None, :]
    same_col = j[:, None] == (c % SUB_BLOCK)[None, :]
    return jnp.asarray(same_head & same_col, dtype=BF16)


def _chunk_tril():
    r = np.arange(TRIL_ROWS)
    same_chunk = (r // GLA_CHUNK)[:, None] == (r // GLA_CHUNK)[None, :]
    return jnp.asarray(same_chunk & (r[None, :] <= r[:, None]), dtype=BF16)


@jax.jit
def _forward(x, p, ln_mix, w_in, w_gate_lr, b_gate, gla_norm, w_pool, pool_scale, w_out,
             ln_ffn, w_ffn_gate, w_ffn_up, w_ffn_down, ln_ple, w_ple_gate, w_ple_proj,
             ln_final):
    bsz, seq, _ = x.shape
    n_tok = bsz * seq
    h = x.reshape(n_tok, D_MODEL)
    assert ln_mix.shape[0] == 1
    tm = TM_MIX
    n_tiles = n_tok // tm
    h = pl.pallas_call(
        functools.partial(_mix_kernel, tm=tm, tiles_per_seq=seq // tm),
        grid=(n_tiles,),
        in_specs=[
            pl.BlockSpec((tm, D_MODEL), lambda n: (jnp.minimum(n + 1, n_tiles - 1), 0)),
            pl.BlockSpec((tm, D_MODEL), lambda n: (n, 0)),
            _const_spec((1, D_MODEL)),
            _const_spec((None,) + w_in.shape[1:]),
            _const_spec((None, GLA_GATE_RANK, GLA_KEY)),
            _const_spec((1, GLA_KEY)),
            _const_spec((TRIL_ROWS, TRIL_ROWS)),
            _const_spec((SUB_BLOCK * GLA_KEY, GLA_KEY)),
            _const_spec((1, GLA_DV)),
            _const_spec((None, len(POOL_WINDOWS), POOL_GC, POOL_GC)),
            _const_spec((1, POOL_WIDTH)),
            _const_spec((None, D_MODEL, D_MODEL)),
        ],
        out_specs=pl.BlockSpec((tm, D_MODEL), lambda n: (n, 0)),
        out_shape=jax.ShapeDtypeStruct((n_tok, D_MODEL), F32),
        scratch_shapes=[
            pltpu.VMEM((D_MODEL, 2 * GLA_KEY + 2 * GLA_VAL + POOL_WIDTH), BF16),
            pltpu.VMEM((D_MODEL, LANES), BF16),
            pltpu.VMEM((LANES, GLA_KEY), BF16),
            pltpu.VMEM((D_MODEL, D_MODEL), BF16),
            pltpu.VMEM((POOL_WIDTH, D_MODEL), BF16),
            pltpu.VMEM((3 * KEY_TILES, tm, LANES), F32),
            pltpu.VMEM((tm, 2 * GLA_VAL), BF16),
            pltpu.VMEM((tm, POOL_WIDTH), F32),
            pltpu.VMEM((3 * KEY_TILES, tm, LANES), F32),
            pltpu.VMEM((tm, 2 * GLA_VAL), BF16),
            pltpu.VMEM((tm, POOL_WIDTH), F32),
            pltpu.VMEM((GLA_DV, GLA_KEY), F32),
            pltpu.VMEM((POOL_HALO + tm, POOL_WIDTH), F32),
            pltpu.VMEM((KEY_TILES, tm, LANES), F32),
            pltpu.VMEM((tm, SUBLANES * GLA_KEY), BF16),
            pltpu.VMEM((tm // 2, SUBLANES * GLA_KEY), BF16),
            pltpu.VMEM((tm, GLA_KEY), F32),
            pltpu.VMEM((tm, GLA_VAL), F32),
        ],
        compiler_params=pltpu.CompilerParams(
            dimension_semantics=("arbitrary",), vmem_limit_bytes=VMEM_LIMIT),
        name="mix",
    )(h, h, ln_mix, w_in, w_gate_lr, b_gate, _chunk_tril(), _selection_matrix(), gla_norm,
      w_pool, pool_scale, w_out)

    h = pl.pallas_call(
        _ffn_kernel,
        grid=(n_tok // TM_FFN,),
        in_specs=[
            pl.BlockSpec((TM_FFN, D_MODEL), lambda r: (r, 0)),
            pl.BlockSpec((TM_FFN, D_PLE), lambda r: (r, 0)),
            _const_spec((1, D_MODEL)),
            _const_spec((D_MODEL, D_FF)),
            _const_spec((D_MODEL, D_FF)),
            _const_spec((D_FF, D_MODEL)),
            _const_spec((1, D_MODEL)),
            _const_spec((D_MODEL, D_MODEL)),
            _const_spec((D_PLE, D_MODEL)),
            _const_spec((1, D_MODEL)),
        ],
        out_specs=pl.BlockSpec((TM_FFN, D_MODEL), lambda r: (r, 0)),
        out_shape=jax.ShapeDtypeStruct((n_tok, D_MODEL), F32),
        scratch_shapes=[pltpu.VMEM((TM_FFN, D_FF), BF16)],
        compiler_params=pltpu.CompilerParams(
            dimension_semantics=("arbitrary",), vmem_limit_bytes=VMEM_LIMIT),
        name="ffn",
    )(h, p[0].reshape(n_tok, D_PLE), ln_ffn[0][None], w_ffn_gate[0].astype(BF16),
      w_ffn_up[0].astype(BF16), w_ffn_down[0].astype(BF16), ln_ple[0][None],
      w_ple_gate[0].astype(BF16), w_ple_proj[0].astype(BF16), ln_final[None])
    return h.reshape(bsz, seq, D_MODEL)


def kernel(x, p, ln_mix, w_in, w_gate_lr, b_gate, gla_norm, w_pool, pool_scale, w_out, ln_ffn,
           w_ffn_gate, w_ffn_up, w_ffn_down, ln_ple, w_ple_gate, w_ple_proj, ln_final):
    return _forward(x, p, ln_mix, w_in, w_gate_lr, b_gate, gla_norm, w_pool, pool_scale, w_out,
                    ln_ffn, w_ffn_gate, w_ffn_up, w_ffn_down, ln_ple, w_ple_gate, w_ple_proj,
                    ln_final)
```

```python
import functools
import math

import jax
import jax.numpy as jnp
import numpy as np
from jax import lax
from jax.experimental import pallas as pl
from jax.experimental.pallas import tpu as pltpu

D_MODEL = 1024
D_PLE = 256
GLA_HEADS = 4
GLA_DK = 64
GLA_DV = 128
GLA_KEY = GLA_HEADS * GLA_DK
GLA_VAL = GLA_HEADS * GLA_DV
GLA_GATE_RANK = 16
GLA_GATE_NORM = 16.0
GLA_CHUNK = 64
POOL_WINDOWS = (2, 4, 8, 16)
POOL_GC = 128
POOL_WIDTH = 512
D_FF = 2816
EPS = 1e-6

LANES = 128
SUBLANES = 8
BF16_TILE_ROWS = 16
KEY_TILES = GLA_KEY // LANES
HEADS_PER_TILE = LANES // GLA_DK
SUB_BLOCK = 16
N_SUB = GLA_CHUNK // SUB_BLOCK
N_CAT = SUB_BLOCK * (N_SUB * (N_SUB - 1) // 2)
TRIL_ROWS = 128
PROJ_COLS = 256
PROJECT_AFTER_PAIR = (1, 4, 7, 11)
PROJECT_AFTER_SCORES = (3, 7)
PROJECT_AFTER_OUTPUTS = (3,)
POOL_HALO = max(POOL_WINDOWS)
FF_CHUNK = 256
TM_MIX = 512
TM_FFN = 1024
VMEM_LIMIT = 56 * 1024 * 1024
LOG2E = math.log2(math.e)

F32 = jnp.float32
BF16 = jnp.bfloat16
NT = (((1,), (1,)), ((), ()))
TN = (((0,), (0,)), ((), ()))


def _dot(a, b, dims=None):
    if dims is None:
        return jnp.dot(a, b, preferred_element_type=F32)
    return lax.dot_general(a, b, dims, preferred_element_type=F32)


def _rmsnorm(x, g):
    return x * lax.rsqrt(jnp.mean(x * x, axis=-1, keepdims=True) + EPS) * g


def _row_bcast(x, row, n):
    return jnp.broadcast_to(x[row:row + 1, :], (n, x.shape[1]))


def _const_spec(shape):
    nd = len(shape)
    return pl.BlockSpec(shape, lambda *_: (0,) * nd, pipeline_mode=pl.Buffered(1))


def _projection_steps(x_ref, ln_ref, wmain_ref, wglr_ref, wgate_ref, bgate_ref,
                      qkg_out, vg_out, u_out):
    hold = {}

    def norm():
        hold["a"] = _rmsnorm(x_ref[...], ln_ref[...]).astype(BF16)

    def piece(col):
        def run():
            z = _dot(hold["a"], wmain_ref[:, col:col + PROJ_COLS])
            for i in range(PROJ_COLS // LANES):
                zi = z[:, i * LANES:(i + 1) * LANES]
                c = col + i * LANES
                if c < GLA_KEY:
                    qkg_out[c // LANES] = zi * (GLA_DK ** -0.5)
                elif c < 2 * GLA_KEY:
                    qkg_out[c // LANES] = zi
                elif c < 2 * GLA_KEY + 2 * GLA_VAL:
                    vg_out[:, c - 2 * GLA_KEY:c - 2 * GLA_KEY + LANES] = zi.astype(BF16)
                else:
                    c -= 2 * GLA_KEY + 2 * GLA_VAL
                    u_out[:, c:c + LANES] = zi
        return run

    def gate():
        g_lr = _dot(hold["a"], wglr_ref[...])
        pre = _dot(g_lr.astype(BF16), wgate_ref[...]) + bgate_ref[...]
        gk = (jnp.minimum(pre, 0.0) - jnp.log1p(jnp.exp(-jnp.abs(pre)))) * (LOG2E / GLA_GATE_NORM)
        for i in range(KEY_TILES):
            qkg_out[2 * KEY_TILES + i] = gk[:, i * LANES:(i + 1) * LANES]

    n_cols = 2 * GLA_KEY + 2 * GLA_VAL + POOL_WIDTH
    return [norm] + [piece(c) for c in range(0, n_cols, PROJ_COLS)] + [gate]


def _mix_kernel(x_next_ref, x_ref, ln_ref, win_t_ref, wgatelr_ref, bgate_ref,
                tril_ref, esel_ref, gnorm_ref, wpool_ref, pscale_ref, wout32_ref,
                wg32_ref, wu32_ref, wd32_ref, wpg32_ref, wpp32_ref,
                h_ref, wg16_ref, wu16_ref, wd16_ref, wpg16_ref, wpp16_ref,
                wmain_ref, wglr_ref, wgate_ref, wout_ref, wcomb_ref,
                qkg_ref, vg_ref, u_ref, qkg_next_ref, vg_next_ref, u_next_ref, state_ref,
                ubuf_ref, b_ref, w1_ref, w2_ref, qoff_ref, o_ref, *, tm, tiles_per_seq):
    for src, dst in ((wg32_ref, wg16_ref), (wu32_ref, wu16_ref), (wd32_ref, wd16_ref),
                     (wpg32_ref, wpg16_ref), (wpp32_ref, wpp16_ref)):
        dst[...] = src[...].astype(BF16)

    n = pl.program_id(0)
    t = n % tiles_per_seq
    n_chunks = tm // GLA_CHUNK
    weights = (ln_ref, wmain_ref, wglr_ref, wgate_ref, bgate_ref)

    @pl.when(n == 0)
    def _():
        s_glr = 2 * GLA_KEY + 2 * GLA_VAL
        for c in range(0, s_glr + POOL_WIDTH, PROJ_COLS):
            r = c if c < s_glr else c + GLA_GATE_RANK
            wmain_ref[:, c:c + PROJ_COLS] = win_t_ref[r:r + PROJ_COLS, :].T.astype(BF16)
        g_lr_t = win_t_ref[s_glr:s_glr + LANES, :].T
        lane = lax.broadcasted_iota(jnp.int32, g_lr_t.shape, 1)
        wglr_ref[...] = jnp.where(lane < GLA_GATE_RANK, g_lr_t, 0.0).astype(BF16)
        wgate_ref[...] = jnp.zeros_like(wgate_ref)
        wgate_ref[0:GLA_GATE_RANK, :] = wgatelr_ref[...].astype(BF16)
        wout_ref[...] = wout32_ref[...].astype(BF16)
        for step in _projection_steps(x_ref, *weights, qkg_ref, vg_ref, u_ref):
            step()
        for g in range(len(POOL_WINDOWS)):
            rows = slice(g * POOL_GC, (g + 1) * POOL_GC)
            scaled = (wpool_ref[g] * pscale_ref[:, rows]).astype(BF16)
            wcomb_ref[rows, :] = _dot(
                scaled, wout_ref[GLA_VAL + g * POOL_GC:GLA_VAL + (g + 1) * POOL_GC, :]).astype(BF16)

    @pl.when(t == 0)
    def _():
        state_ref[...] = jnp.zeros_like(state_ref)
        ubuf_ref[0:POOL_HALO, :] = jnp.zeros((POOL_HALO, POOL_WIDTH), F32)

    pending = _projection_steps(x_next_ref, *weights, qkg_next_ref, vg_next_ref, u_next_ref)

    def project_next(k):
        for _ in range(min(k, len(pending))):
            pending.pop(0)()

    vg = vg_ref
    project_next(1)

    def q_tile(i):
        return qkg_ref.at[i]

    def k_tile(i):
        return qkg_ref.at[KEY_TILES + i]

    gk = jnp.concatenate([qkg_ref[2 * KEY_TILES + i] for i in range(KEY_TILES)], axis=1)
    g_hi = gk.astype(BF16)
    rem = gk - g_hi.astype(F32)
    g_mid = rem.astype(BF16)
    g_lo = (rem - g_mid.astype(F32)).astype(BF16)
    tril = tril_ref[...]
    for r0 in range(0, tm, TRIL_ROWS):
        rs = slice(r0, r0 + TRIL_ROWS)
        b_blk = _dot(tril, g_hi[rs]) + _dot(tril, g_mid[rs]) + _dot(tril, g_lo[rs])
        for i in range(KEY_TILES):
            b_ref[i, rs, :] = b_blk[:, i * LANES:(i + 1) * LANES]

    row8 = lax.broadcasted_iota(jnp.int32, (SUBLANES, LANES), 0)
    neg_inf = jnp.full((SUBLANES, LANES), -jnp.inf, F32)

    def pair_body(p, first_pass):
        half = tm // 2
        for i in range(KEY_TILES):
            ld = lambda ref, g, hi: ref[pl.ds(g * SUB_BLOCK + hi * SUBLANES, SUBLANES), :]
            groups = (2 * p, 2 * p + 1)
            src = 0 if first_pass else 1
            b_hi = [ld(b_ref.at[i], g, 1) for g in groups]
            q_hi = [ld(q_tile(i), g, 1) for g in groups]
            b_src = [ld(b_ref.at[i], g, src) for g in groups]
            k_src = [ld(k_tile(i), g, src) for g in groups]
            if first_pass:
                q_lo = [ld(q_tile(i), g, 0) for g in groups]
            for s in range(SUBLANES):
                w_lo, w_hi = [], []
                for n_g, g in enumerate(groups):
                    b_j = _row_bcast(b_src[n_g], s, SUBLANES)
                    k_j = _row_bcast(k_src[n_g], s, SUBLANES)
                    if first_pass:
                        e_lo = jnp.exp2(jnp.where(row8 >= s, b_src[n_g] - b_j, neg_inf))
                        e_hi = jnp.exp2(b_hi[n_g] - b_j)
                        w_lo.append(q_lo[n_g] * k_j * e_lo)
                    else:
                        e_hi = jnp.exp2(jnp.where(row8 >= s, b_hi[n_g] - b_j, neg_inf))
                    w_hi.append(q_hi[n_g] * k_j * e_hi)
                    if first_pass and s == 0:
                        qoff_ref[pl.ds(g * SUB_BLOCK, SUB_BLOCK), i * LANES:(i + 1) * LANES] = (
                            jnp.concatenate([q_lo[n_g] * e_lo, q_hi[n_g] * e_hi], axis=0))
                col = s * GLA_KEY + i * LANES
                w_hi = jnp.concatenate(w_hi, axis=0).astype(BF16)
                if first_pass:
                    w1_ref[pl.ds(p * SUB_BLOCK, SUB_BLOCK), col:col + LANES] = (
                        jnp.concatenate(w_lo, axis=0).astype(BF16))
                    w1_ref[pl.ds(half + p * SUB_BLOCK, SUB_BLOCK), col:col + LANES] = w_hi
                else:
                    w2_ref[pl.ds(p * SUB_BLOCK, SUB_BLOCK), col:col + LANES] = w_hi

    n_grp = tm // SUB_BLOCK
    half_k = SUBLANES * GLA_KEY
    for p in range(n_grp // 2):
        pair_body(p, True)
        if p in PROJECT_AFTER_PAIR:
            project_next(1)
    a_first = _dot(w1_ref[...], esel_ref[0:half_k, :])
    for p in range(n_grp // 2):
        pair_body(p, False)
    a_lo = a_first[0:tm // 2]
    a_hi = a_first[tm // 2:] + _dot(w2_ref[...], esel_ref[half_k:, :])
    a_diag = jnp.concatenate(
        [a_lo.reshape(n_grp, SUBLANES, GLA_KEY), a_hi.reshape(n_grp, SUBLANES, GLA_KEY)],
        axis=1).reshape(tm, GLA_KEY)

    lane_head = (lax.broadcasted_iota(jnp.int32, (tm, GLA_KEY), 1) % LANES) // GLA_DK
    blk_row = (lax.broadcasted_iota(jnp.int32, (tm, GLA_KEY), 0) % GLA_CHUNK) // SUB_BLOCK
    blk_lane = (lax.broadcasted_iota(jnp.int32, (tm, GLA_KEY), 1) % GLA_CHUNK) // SUB_BLOCK
    b_all = jnp.concatenate([b_ref[i] for i in range(KEY_TILES)], axis=1)
    q_all = jnp.concatenate([q_tile(i)[...] for i in range(KEY_TILES)], axis=1)
    q_in = (q_all * jnp.exp2(b_all)).astype(BF16)
    a_dg = jnp.where(blk_row == blk_lane, a_diag, 0.0).astype(BF16)
    q_off_all = qoff_ref[...]
    q_off = [jnp.where(lane_head == hh, q_off_all, 0.0).astype(BF16)
             for hh in range(HEADS_PER_TILE)]

    stacked = HEADS_PER_TILE * GLA_CHUNK
    prow = (lax.broadcasted_iota(jnp.int32, (stacked, N_CAT), 0) % GLA_CHUNK) // SUB_BLOCK
    pcol = lax.broadcasted_iota(jnp.int32, (stacked, N_CAT), 1)
    pgrp = jnp.zeros_like(pcol)
    start = 0
    for i in range(1, N_SUB):
        pgrp = jnp.where((pcol >= start) & (pcol < start + i * SUB_BLOCK), i, pgrp)
        start += i * SUB_BLOCK
    off_mask = prow == pgrp
    first_head_lanes = lax.broadcasted_iota(jnp.int32, (GLA_DV, LANES), 1) < GLA_DK
    first_head_rows = lax.broadcasted_iota(jnp.int32, (LANES, GLA_DV), 0) < GLA_DK
    zeros_v = jnp.zeros((GLA_CHUNK, GLA_DV), BF16)

    p_off = {}
    d_state = {}
    decay = {}
    for c in range(n_chunks):
        r0 = c * GLA_CHUNK
        rs = slice(r0, r0 + GLA_CHUNK)
        for i in range(KEY_TILES):
            ls = slice(i * LANES, (i + 1) * LANES)
            bc = b_ref[i, rs, :]
            kc = k_tile(i)[rs, :]
            b_last = _row_bcast(bc, GLA_CHUNK - 1, GLA_CHUNK)
            k_dec = (kc * jnp.exp2(b_last - bc)).astype(BF16)
            decay[c, i] = jnp.exp2(b_last[0:SUBLANES])
            pieces = []
            for s in range(1, N_SUB):
                m = s * SUB_BLOCK
                r_s = _row_bcast(bc, m, m)
                pieces.append(kc[0:m] * jnp.exp2(r_s - bc[0:m]))
            k_cat = jnp.concatenate(pieces, axis=0).astype(BF16)
            q_two = jnp.concatenate([q[rs, ls] for q in q_off], axis=0)
            p_off[c, i] = jnp.where(off_mask, _dot(q_two, k_cat, NT), 0.0).astype(BF16)
            ds = _dot(vg[rs, i * HEADS_PER_TILE * GLA_DV:(i + 1) * HEADS_PER_TILE * GLA_DV],
                      k_dec, TN)
            d_state[c, i] = jnp.where(first_head_lanes, ds[0:GLA_DV], ds[GLA_DV:])
        if c in PROJECT_AFTER_SCORES:
            project_next(1)

    state_in = {}
    for i in range(KEY_TILES):
        ls = slice(i * LANES, (i + 1) * LANES)
        s = state_ref[:, ls]
        for c in range(n_chunks):
            s_t = s.T
            state_in[c, i] = jnp.concatenate(
                [jnp.where(first_head_rows, s_t, 0.0), jnp.where(first_head_rows, 0.0, s_t)],
                axis=1).astype(BF16)
            s = (s.reshape(GLA_DV // SUBLANES, SUBLANES, LANES) * decay[c, i][None]
                 ).reshape(GLA_DV, LANES) + d_state[c, i]
        state_ref[:, ls] = s

    for c in range(n_chunks):
        r0 = c * GLA_CHUNK
        rs = slice(r0, r0 + GLA_CHUNK)
        for i in range(KEY_TILES):
            ls = slice(i * LANES, (i + 1) * LANES)
            vs = slice(i * HEADS_PER_TILE * GLA_DV, (i + 1) * HEADS_PER_TILE * GLA_DV)
            v_heads = [vg[rs, (i * HEADS_PER_TILE + hh) * GLA_DV:(i * HEADS_PER_TILE + hh + 1) * GLA_DV]
                       for hh in range(HEADS_PER_TILE)]
            off = []
            for hh, vc in enumerate(v_heads):
                v_cat = jnp.concatenate([vc[0:s * SUB_BLOCK] for s in range(1, N_SUB)], axis=0)
                off.append(_dot(p_off[c, i][hh * GLA_CHUNK:(hh + 1) * GLA_CHUNK], v_cat))
            v_diag = jnp.concatenate(
                [jnp.concatenate([v_heads[0], zeros_v], axis=1),
                 jnp.concatenate([zeros_v, v_heads[1]], axis=1)], axis=0)
            o_ref[rs, vs] = (jnp.concatenate(off, axis=1) + _dot(a_dg[rs, ls], v_diag)
                             + _dot(q_in[rs, ls], state_in[c, i]))
        if c in PROJECT_AFTER_OUTPUTS:
            project_next(1)

    gated = []
    for h in range(GLA_HEADS):
        vs = slice(h * GLA_DV, (h + 1) * GLA_DV)
        o_h = _rmsnorm(o_ref[:, vs], gnorm_ref[...])
        g_h = vg[:, GLA_VAL + h * GLA_DV:GLA_VAL + (h + 1) * GLA_DV].astype(F32)
        gated.append((o_h * (g_h * jax.nn.sigmoid(g_h))).astype(BF16))
    project_next(len(pending))
    mix = _dot(jnp.concatenate(gated, axis=1), wout_ref[0:GLA_VAL, :])

    u = u_ref[...]
    ubuf_ref[POOL_HALO:POOL_HALO + tm, :] = u
    pos = t * tm + lax.broadcasted_iota(jnp.int32, (tm, 1), 0)
    pooled = []
    for g, w in enumerate(POOL_WINDOWS):
        ls = slice(g * POOL_GC, (g + 1) * POOL_GC)
        acc = ubuf_ref[:, ls]
        lo, step = 0, 1
        while step < w:
            acc = acc[step:] + acc[:-step]
            lo += step
            step *= 2
        acc = acc[POOL_HALO - lo:]
        cnt = jnp.minimum(pos + 1, w).astype(F32)
        pooled.append((acc / cnt - u[:, ls]).astype(BF16))
    ubuf_ref[0:POOL_HALO, :] = ubuf_ref[tm:tm + POOL_HALO, :]
    mix = mix + _dot(jnp.concatenate(pooled, axis=1), wcomb_ref[...])

    h_ref[...] = x_ref[...] + mix

    qkg_ref[...] = qkg_next_ref[...]
    vg_ref[...] = vg_next_ref[...]
    u_ref[...] = u_next_ref[...]


def _ffn_kernel(h_ref, p_ref, lnf_ref, wg_ref, wu_ref, wd_ref, lnp_ref, wpg_ref, wpp_ref,
                lnfin_ref, out_ref, hid_ref):
    ple = _dot(p_ref[...].astype(BF16), wpp_ref[...])
    h = h_ref[...]
    f = _rmsnorm(h, lnf_ref[...]).astype(BF16)
    for c in range(0, D_FF, FF_CHUNK):
        g = _dot(f, wg_ref[:, c:c + FF_CHUNK])
        u = _dot(f, wu_ref[:, c:c + FF_CHUNK])
        hid_ref[:, c:c + FF_CHUNK] = (g * jax.nn.sigmoid(g) * u).astype(BF16)
    half = h.shape[0] // 2
    halves = (slice(0, half), slice(half, 2 * half))
    mid = [h[rows] + _dot(hid_ref[rows, :], wd_ref[...]) for rows in halves]
    gates = [jax.nn.sigmoid(_dot(_rmsnorm(x, lnp_ref[...]).astype(BF16), wpg_ref[...]))
             for x in mid]
    for rows, x, gate in zip(halves, mid, gates):
        out_ref[rows, :] = _rmsnorm(x + gate * ple[rows], lnfin_ref[...])


def _selection_matrix():
    r = np.arange(SUB_BLOCK * GLA_KEY)
    j, hd = r // GLA_KEY, r % GLA_KEY
    c = np.arange(GLA_KEY)
    same_head = (hd // GLA_DK)[:, None] == (c // GLA_CHUNK)[None, :]
    same_col = j[:, None] == (c % SUB_BLOCK)[None, :]
    return jnp.asarray(same_head & same_col, dtype=BF16)


def _slab_specs(rows, cols, n_steps):
    slab = next(s for s in range(BF16_TILE_ROWS, rows + 1, BF16_TILE_ROWS)
                if rows % s == 0 and s * n_steps >= rows)
    last = rows // slab - 1
    return (pl.BlockSpec((None, slab, cols), lambda n: (0, jnp.minimum(n, last), 0)),
            pl.BlockSpec((slab, cols), lambda n: (jnp.minimum(n, last), 0)))


def _chunk_tril():
    r = np.arange(TRIL_ROWS)
    same_chunk = (r // GLA_CHUNK)[:, None] == (r // GLA_CHUNK)[None, :]
    return jnp.asarray(same_chunk & (r[None, :] <= r[:, None]), dtype=BF16)


@jax.jit
def _forward(x, p, ln_mix, w_in, w_gate_lr, b_gate, gla_norm, w_pool, pool_scale, w_out,
             ln_ffn, w_ffn_gate, w_ffn_up, w_ffn_down, ln_ple, w_ple_gate, w_ple_proj,
             ln_final):
    bsz, seq, _ = x.shape
    n_tok = bsz * seq
    h = x.reshape(n_tok, D_MODEL)
    assert ln_mix.shape[0] == 1
    tm = TM_MIX
    n_tiles = n_tok // tm
    ffn_weights = (w_ffn_gate, w_ffn_up, w_ffn_down, w_ple_gate, w_ple_proj)
    slabs = [_slab_specs(w.shape[1], w.shape[2], n_tiles) for w in ffn_weights]
    h, *ffn_w16 = pl.pallas_call(
        functools.partial(_mix_kernel, tm=tm, tiles_per_seq=seq // tm),
        grid=(n_tiles,),
        in_specs=[
            pl.BlockSpec((tm, D_MODEL), lambda n: (jnp.minimum(n + 1, n_tiles - 1), 0)),
            pl.BlockSpec((tm, D_MODEL), lambda n: (n, 0)),
            _const_spec((1, D_MODEL)),
            _const_spec((None, w_in.shape[2], w_in.shape[1])),
            _const_spec((None, GLA_GATE_RANK, GLA_KEY)),
            _const_spec((1, GLA_KEY)),
            _const_spec((TRIL_ROWS, TRIL_ROWS)),
            _const_spec((SUB_BLOCK * GLA_KEY, GLA_KEY)),
            _const_spec((1, GLA_DV)),
            _const_spec((None, len(POOL_WINDOWS), POOL_GC, POOL_GC)),
            _const_spec((1, POOL_WIDTH)),
            _const_spec((None, D_MODEL, D_MODEL)),
        ] + [src for src, _ in slabs],
        out_specs=[pl.BlockSpec((tm, D_MODEL), lambda n: (n, 0))] + [dst for _, dst in slabs],
        out_shape=[jax.ShapeDtypeStruct((n_tok, D_MODEL), F32)]
        + [jax.ShapeDtypeStruct(w.shape[1:], BF16) for w in ffn_weights],
        scratch_shapes=[
            pltpu.VMEM((D_MODEL, 2 * GLA_KEY + 2 * GLA_VAL + POOL_WIDTH), BF16),
            pltpu.VMEM((D_MODEL, LANES), BF16),
            pltpu.VMEM((LANES, GLA_KEY), BF16),
            pltpu.VMEM((D_MODEL, D_MODEL), BF16),
            pltpu.VMEM((POOL_WIDTH, D_MODEL), BF16),
            pltpu.VMEM((3 * KEY_TILES, tm, LANES), F32),
            pltpu.VMEM((tm, 2 * GLA_VAL), BF16),
            pltpu.VMEM((tm, POOL_WIDTH), F32),
            pltpu.VMEM((3 * KEY_TILES, tm, LANES), F32),
            pltpu.VMEM((tm, 2 * GLA_VAL), BF16),
            pltpu.VMEM((tm, POOL_WIDTH), F32),
            pltpu.VMEM((GLA_DV, GLA_KEY), F32),
            pltpu.VMEM((POOL_HALO + tm, POOL_WIDTH), F32),
            pltpu.VMEM((KEY_TILES, tm, LANES), F32),
            pltpu.VMEM((tm, SUBLANES * GLA_KEY), BF16),
            pltpu.VMEM((tm // 2, SUBLANES * GLA_KEY), BF16),
            pltpu.VMEM((tm, GLA_KEY), F32),
            pltpu.VMEM((tm, GLA_VAL), F32),
        ],
        compiler_params=pltpu.CompilerParams(
            dimension_semantics=("arbitrary",), vmem_limit_bytes=VMEM_LIMIT),
        name="mix",
    )(h, h, ln_mix, jnp.swapaxes(w_in, 1, 2), w_gate_lr, b_gate, _chunk_tril(),
      _selection_matrix(), gla_norm, w_pool, pool_scale, w_out, *ffn_weights)

    h = pl.pallas_call(
        _ffn_kernel,
        grid=(n_tok // TM_FFN,),
        in_specs=[
            pl.BlockSpec((TM_FFN, D_MODEL), lambda r: (r, 0)),
            pl.BlockSpec((TM_FFN, D_PLE), lambda r: (r, 0)),
            _const_spec((1, D_MODEL)),
            _const_spec((D_MODEL, D_FF)),
            _const_spec((D_MODEL, D_FF)),
            _const_spec((D_FF, D_MODEL)),
            _const_spec((1, D_MODEL)),
            _const_spec((D_MODEL, D_MODEL)),
            _const_spec((D_PLE, D_MODEL)),
            _const_spec((1, D_MODEL)),
        ],
        out_specs=pl.BlockSpec((TM_FFN, D_MODEL), lambda r: (r, 0)),
        out_shape=jax.ShapeDtypeStruct((n_tok, D_MODEL), F32),
        scratch_shapes=[pltpu.VMEM((TM_FFN, D_FF), BF16)],
        compiler_params=pltpu.CompilerParams(
            dimension_semantics=("arbitrary",), vmem_limit_bytes=VMEM_LIMIT),
        name="ffn",
    )(h, p.reshape(n_tok, D_PLE), ln_ffn, ffn_w16[0], ffn_w16[1], ffn_w16[2], ln_ple,
      ffn_w16[3], ffn_w16[4], ln_final[None])
    return h.reshape(bsz, seq, D_MODEL)


def kernel(x, p, ln_mix, w_in, w_gate_lr, b_gate, gla_norm, w_pool, pool_scale, w_out, ln_ffn,
           w_ffn_gate, w_ffn_up, w_ffn_down, ln_ple, w_ple_gate, w_ple_proj, ln_final):
    return _forward(x, p, ln_mix, w_in, w_gate_lr, b_gate, gla_norm, w_pool, pool_scale, w_out,
                    ln_ffn, w_ffn_gate, w_ffn_up, w_ffn_down, ln_ple, w_ple_gate, w_ple_proj,
                    ln_final)
```

```python
import functools
import math

import jax
import jax.numpy as jnp
import numpy as np
from jax import lax
from jax.experimental import pallas as pl
from jax.experimental.pallas import tpu as pltpu

D_MODEL = 1024
D_PLE = 256
GLA_HEADS = 4
GLA_DK = 64
GLA_DV = 128
GLA_KEY = GLA_HEADS * GLA_DK
GLA_VAL = GLA_HEADS * GLA_DV
GLA_GATE_RANK = 16
GLA_GATE_NORM = 16.0
GLA_CHUNK = 64
POOL_WINDOWS = (2, 4, 8, 16)
POOL_GC = 128
POOL_WIDTH = 512
D_FF = 2816
EPS = 1e-6

LANES = 128
SUBLANES = 8
BF16_TILE_ROWS = 16
KEY_TILES = GLA_KEY // LANES
HEADS_PER_TILE = LANES // GLA_DK
SUB_BLOCK = 16
N_SUB = GLA_CHUNK // SUB_BLOCK
N_CAT = SUB_BLOCK * (N_SUB * (N_SUB - 1) // 2)
TRIL_ROWS = 128
PROJ_COLS = 256
PROJECT_AFTER_PAIR = (1, 4, 7, 11)
PROJECT_AFTER_SCORES = (3, 7)
PROJECT_AFTER_OUTPUTS = (3,)
POOL_HALO = max(POOL_WINDOWS)
FF_CHUNK = 256
TM_MIX = 512
TM_FFN = 1024
VMEM_LIMIT = 56 * 1024 * 1024
LOG2E = math.log2(math.e)

F32 = jnp.float32
BF16 = jnp.bfloat16
NT = (((1,), (1,)), ((), ()))
TN = (((0,), (0,)), ((), ()))


def _dot(a, b, dims=None):
    if dims is None:
        return jnp.dot(a, b, preferred_element_type=F32)
    return lax.dot_general(a, b, dims, preferred_element_type=F32)


def _rmsnorm(x, g):
    return x * lax.rsqrt(jnp.mean(x * x, axis=-1, keepdims=True) + EPS) * g


def _row_bcast(x, row, n):
    return jnp.broadcast_to(x[row:row + 1, :], (n, x.shape[1]))


def _const_spec(shape):
    nd = len(shape)
    return pl.BlockSpec(shape, lambda *_: (0,) * nd, pipeline_mode=pl.Buffered(1))


def _projection_steps(x_ref, ln_ref, wmain_ref, wgfold_ref, bgate_ref,
                      qkg_out, vg_out, u_out):
    hold = {}

    def norm():
        hold["a"] = _rmsnorm(x_ref[...], ln_ref[...]).astype(BF16)

    def piece(col):
        def run():
            z = _dot(hold["a"], wmain_ref[:, col:col + PROJ_COLS])
            for i in range(PROJ_COLS // LANES):
                zi = z[:, i * LANES:(i + 1) * LANES]
                c = col + i * LANES
                if c < GLA_KEY:
                    qkg_out[c // LANES] = zi * (GLA_DK ** -0.5)
                elif c < 2 * GLA_KEY:
                    qkg_out[c // LANES] = zi
                elif c < 2 * GLA_KEY + 2 * GLA_VAL:
                    vg_out[:, c - 2 * GLA_KEY:c - 2 * GLA_KEY + LANES] = zi.astype(BF16)
                else:
                    c -= 2 * GLA_KEY + 2 * GLA_VAL
                    u_out[:, c:c + LANES] = zi
        return run

    def gate():
        pre = _dot(hold["a"], wgfold_ref[...]) + bgate_ref[...]
        gk = (jnp.minimum(pre, 0.0) - jnp.log1p(jnp.exp(-jnp.abs(pre)))) * (LOG2E / GLA_GATE_NORM)
        for i in range(KEY_TILES):
            qkg_out[2 * KEY_TILES + i] = gk[:, i * LANES:(i + 1) * LANES]

    n_cols = 2 * GLA_KEY + 2 * GLA_VAL + POOL_WIDTH
    return [norm] + [piece(c) for c in range(0, n_cols, PROJ_COLS)] + [gate]


def _mix_kernel(x_next_ref, x_ref, ln_ref, win_t_ref, wgatelr_ref, bgate_ref,
                tril_ref, esel_ref, gnorm_ref, wpool_ref, pscale_ref, wout32_ref,
                wg32_ref, wu32_ref, wd32_ref, wpg32_ref, wpp32_ref,
                h_ref, wg16_ref, wu16_ref, wd16_ref, wpg16_ref, wpp16_ref,
                wmain_ref, wgfold_ref, wout_ref, wcomb_ref,
                qkg_ref, vg_ref, u_ref, qkg_next_ref, vg_next_ref, u_next_ref, state_ref,
                ubuf_ref, b_ref, w1_ref, w2_ref, qoff_ref, o_ref, *, tm, tiles_per_seq):
    for src, dst in ((wg32_ref, wg16_ref), (wu32_ref, wu16_ref), (wd32_ref, wd16_ref),
                     (wpg32_ref, wpg16_ref), (wpp32_ref, wpp16_ref)):
        dst[...] = src[...].astype(BF16)

    n = pl.program_id(0)
    t = n % tiles_per_seq
    n_chunks = tm // GLA_CHUNK
    weights = (ln_ref, wmain_ref, wgfold_ref, bgate_ref)

    @pl.when(n == 0)
    def _():
        s_glr = 2 * GLA_KEY + 2 * GLA_VAL
        for c in range(0, s_glr + POOL_WIDTH, PROJ_COLS):
            r = c if c < s_glr else c + GLA_GATE_RANK
            wmain_ref[:, c:c + PROJ_COLS] = win_t_ref[r:r + PROJ_COLS, :].T.astype(BF16)
        g_lr_t = win_t_ref[s_glr:s_glr + LANES, :].T
        lane = lax.broadcasted_iota(jnp.int32, g_lr_t.shape, 1)
        w_glr = jnp.where(lane < GLA_GATE_RANK, g_lr_t, 0.0).astype(BF16)
        w_gate = jnp.concatenate(
            [wgatelr_ref[...].astype(BF16),
             jnp.zeros((LANES - GLA_GATE_RANK, GLA_KEY), BF16)], axis=0)
        wgfold_ref[...] = _dot(w_glr, w_gate).astype(BF16)
        wout_ref[...] = wout32_ref[...].astype(BF16)
        for step in _projection_steps(x_ref, *weights, qkg_ref, vg_ref, u_ref):
            step()
        for g in range(len(POOL_WINDOWS)):
            rows = slice(g * POOL_GC, (g + 1) * POOL_GC)
            scaled = (wpool_ref[g] * pscale_ref[:, rows]).astype(BF16)
            wcomb_ref[rows, :] = _dot(
                scaled, wout_ref[GLA_VAL + g * POOL_GC:GLA_VAL + (g + 1) * POOL_GC, :]).astype(BF16)

    @pl.when(t == 0)
    def _():
        state_ref[...] = jnp.zeros_like(state_ref)
        ubuf_ref[0:POOL_HALO, :] = jnp.zeros((POOL_HALO, POOL_WIDTH), F32)

    pending = _projection_steps(x_next_ref, *weights, qkg_next_ref, vg_next_ref, u_next_ref)

    def project_next(k):
        for _ in range(min(k, len(pending))):
            pending.pop(0)()

    vg = vg_ref
    project_next(1)

    def q_tile(i):
        return qkg_ref.at[i]

    def k_tile(i):
        return qkg_ref.at[KEY_TILES + i]

    gk = jnp.concatenate([qkg_ref[2 * KEY_TILES + i] for i in range(KEY_TILES)], axis=1)
    g_hi = gk.astype(BF16)
    g_lo = (gk - g_hi.astype(F32)).astype(BF16)
    tril = tril_ref[...]
    for r0 in range(0, tm, TRIL_ROWS):
        rs = slice(r0, r0 + TRIL_ROWS)
        b_blk = _dot(tril, g_hi[rs]) + _dot(tril, g_lo[rs])
        for i in range(KEY_TILES):
            b_ref[i, rs, :] = b_blk[:, i * LANES:(i + 1) * LANES]

    row8 = lax.broadcasted_iota(jnp.int32, (SUBLANES, LANES), 0)
    neg_inf = jnp.full((SUBLANES, LANES), -jnp.inf, F32)

    def pair_body(p, first_pass):
        half = tm // 2
        for i in range(KEY_TILES):
            ld = lambda ref, g, hi: ref[pl.ds(g * SUB_BLOCK + hi * SUBLANES, SUBLANES), :]
            groups = (2 * p, 2 * p + 1)
            src = 0 if first_pass else 1
            b_hi = [ld(b_ref.at[i], g, 1) for g in groups]
            q_hi = [ld(q_tile(i), g, 1) for g in groups]
            b_src = [ld(b_ref.at[i], g, src) for g in groups]
            k_src = [ld(k_tile(i), g, src) for g in groups]
            if first_pass:
                q_lo = [ld(q_tile(i), g, 0) for g in groups]
            for s in range(SUBLANES):
                w_lo, w_hi = [], []
                for n_g, g in enumerate(groups):
                    b_j = _row_bcast(b_src[n_g], s, SUBLANES)
                    k_j = _row_bcast(k_src[n_g], s, SUBLANES)
                    if first_pass:
                        e_lo = jnp.exp2(jnp.where(row8 >= s, b_src[n_g] - b_j, neg_inf))
                        e_hi = jnp.exp2(b_hi[n_g] - b_j)
                        w_lo.append(q_lo[n_g] * k_j * e_lo)
                    else:
                        e_hi = jnp.exp2(jnp.where(row8 >= s, b_hi[n_g] - b_j, neg_inf))
                    w_hi.append(q_hi[n_g] * k_j * e_hi)
                    if first_pass and s == 0:
                        qoff_ref[pl.ds(g * SUB_BLOCK, SUB_BLOCK), i * LANES:(i + 1) * LANES] = (
                            jnp.concatenate([q_lo[n_g] * e_lo, q_hi[n_g] * e_hi], axis=0))
                col = s * GLA_KEY + i * LANES
                w_hi = jnp.concatenate(w_hi, axis=0).astype(BF16)
                if first_pass:
                    w1_ref[pl.ds(p * SUB_BLOCK, SUB_BLOCK), col:col + LANES] = (
                        jnp.concatenate(w_lo, axis=0).astype(BF16))
                    w1_ref[pl.ds(half + p * SUB_BLOCK, SUB_BLOCK), col:col + LANES] = w_hi
                else:
                    w2_ref[pl.ds(p * SUB_BLOCK, SUB_BLOCK), col:col + LANES] = w_hi

    n_grp = tm // SUB_BLOCK
    half_k = SUBLANES * GLA_KEY
    for p in range(n_grp // 2):
        pair_body(p, True)
        if p in PROJECT_AFTER_PAIR:
            project_next(1)
    a_first = _dot(w1_ref[...], esel_ref[0:half_k, :])
    for p in range(n_grp // 2):
        pair_body(p, False)
    a_lo = a_first[0:tm // 2]
    a_hi = a_first[tm // 2:] + _dot(w2_ref[...], esel_ref[half_k:, :])
    a_diag = jnp.concatenate(
        [a_lo.reshape(n_grp, SUBLANES, GLA_KEY), a_hi.reshape(n_grp, SUBLANES, GLA_KEY)],
        axis=1).reshape(tm, GLA_KEY)

    lane_head = (lax.broadcasted_iota(jnp.int32, (tm, GLA_KEY), 1) % LANES) // GLA_DK
    blk_row = (lax.broadcasted_iota(jnp.int32, (tm, GLA_KEY), 0) % GLA_CHUNK) // SUB_BLOCK
    blk_lane = (lax.broadcasted_iota(jnp.int32, (tm, GLA_KEY), 1) % GLA_CHUNK) // SUB_BLOCK
    b_all = jnp.concatenate([b_ref[i] for i in range(KEY_TILES)], axis=1)
    q_all = jnp.concatenate([q_tile(i)[...] for i in range(KEY_TILES)], axis=1)
    q_in = (q_all * jnp.exp2(b_all)).astype(BF16)
    a_dg = jnp.where(blk_row == blk_lane, a_diag, 0.0).astype(BF16)
    q_off_all = qoff_ref[...]
    q_off = [jnp.where(lane_head == hh, q_off_all, 0.0).astype(BF16)
             for hh in range(HEADS_PER_TILE)]

    stacked = HEADS_PER_TILE * GLA_CHUNK
    prow = (lax.broadcasted_iota(jnp.int32, (stacked, N_CAT), 0) % GLA_CHUNK) // SUB_BLOCK
    pcol = lax.broadcasted_iota(jnp.int32, (stacked, N_CAT), 1)
    pgrp = jnp.zeros_like(pcol)
    start = 0
    for i in range(1, N_SUB):
        pgrp = jnp.where((pcol >= start) & (pcol < start + i * SUB_BLOCK), i, pgrp)
        start += i * SUB_BLOCK
    off_mask = prow == pgrp
    first_head_lanes = lax.broadcasted_iota(jnp.int32, (GLA_DV, LANES), 1) < GLA_DK
    first_head_rows = lax.broadcasted_iota(jnp.int32, (LANES, GLA_DV), 0) < GLA_DK
    zeros_v = jnp.zeros((GLA_CHUNK, GLA_DV), BF16)

    p_off = {}
    d_state = {}
    decay = {}
    for c in range(n_chunks):
        r0 = c * GLA_CHUNK
        rs = slice(r0, r0 + GLA_CHUNK)
        for i in range(KEY_TILES):
            ls = slice(i * LANES, (i + 1) * LANES)
            bc = b_ref[i, rs, :]
            kc = k_tile(i)[rs, :]
            b_last = _row_bcast(bc, GLA_CHUNK - 1, GLA_CHUNK)
            k_dec = (kc * jnp.exp2(b_last - bc)).astype(BF16)
            decay[c, i] = jnp.exp2(b_last[0:SUBLANES])
            pieces = []
            for s in range(1, N_SUB):
                m = s * SUB_BLOCK
                r_s = _row_bcast(bc, m, m)
                pieces.append(kc[0:m] * jnp.exp2(r_s - bc[0:m]))
            k_cat = jnp.concatenate(pieces, axis=0).astype(BF16)
            q_two = jnp.concatenate([q[rs, ls] for q in q_off], axis=0)
            p_off[c, i] = jnp.where(off_mask, _dot(q_two, k_cat, NT), 0.0).astype(BF16)
            ds = _dot(vg[rs, i * HEADS_PER_TILE * GLA_DV:(i + 1) * HEADS_PER_TILE * GLA_DV],
                      k_dec, TN)
            d_state[c, i] = jnp.where(first_head_lanes, ds[0:GLA_DV], ds[GLA_DV:])
        if c in PROJECT_AFTER_SCORES:
            project_next(1)

    state_in = {}
    for i in range(KEY_TILES):
        ls = slice(i * LANES, (i + 1) * LANES)
        s = state_ref[:, ls]
        for c in range(n_chunks):
            s_t = s.T
            state_in[c, i] = jnp.concatenate(
                [jnp.where(first_head_rows, s_t, 0.0), jnp.where(first_head_rows, 0.0, s_t)],
                axis=1).astype(BF16)
            s = (s.reshape(GLA_DV // SUBLANES, SUBLANES, LANES) * decay[c, i][None]
                 ).reshape(GLA_DV, LANES) + d_state[c, i]
        state_ref[:, ls] = s

    for c in range(n_chunks):
        r0 = c * GLA_CHUNK
        rs = slice(r0, r0 + GLA_CHUNK)
        for i in range(KEY_TILES):
            ls = slice(i * LANES, (i + 1) * LANES)
            vs = slice(i * HEADS_PER_TILE * GLA_DV, (i + 1) * HEADS_PER_TILE * GLA_DV)
            v_heads = [vg[rs, (i * HEADS_PER_TILE + hh) * GLA_DV:(i * HEADS_PER_TILE + hh + 1) * GLA_DV]
                       for hh in range(HEADS_PER_TILE)]
            off = []
            for hh, vc in enumerate(v_heads):
                v_cat = jnp.concatenate([vc[0:s * SUB_BLOCK] for s in range(1, N_SUB)], axis=0)
                off.append(_dot(p_off[c, i][hh * GLA_CHUNK:(hh + 1) * GLA_CHUNK], v_cat))
            v_diag = jnp.concatenate(
                [jnp.concatenate([v_heads[0], zeros_v], axis=1),
                 jnp.concatenate([zeros_v, v_heads[1]], axis=1)], axis=0)
            o_ref[rs, vs] = (jnp.concatenate(off, axis=1) + _dot(a_dg[rs, ls], v_diag)
                             + _dot(q_in[rs, ls], state_in[c, i]))
        if c in PROJECT_AFTER_OUTPUTS:
            project_next(1)

    gated = []
    for h in range(GLA_HEADS):
        vs = slice(h * GLA_DV, (h + 1) * GLA_DV)
        o_h = _rmsnorm(o_ref[:, vs], gnorm_ref[...])
        g_h = vg[:, GLA_VAL + h * GLA_DV:GLA_VAL + (h + 1) * GLA_DV].astype(F32)
        gated.append((o_h * (g_h * jax.nn.sigmoid(g_h))).astype(BF16))
    project_next(len(pending))
    mix = _dot(jnp.concatenate(gated, axis=1), wout_ref[0:GLA_VAL, :])

    u = u_ref[...]
    ubuf_ref[POOL_HALO:POOL_HALO + tm, :] = u
    pos = t * tm + lax.broadcasted_iota(jnp.int32, (tm, 1), 0)
    pooled = []
    for g, w in enumerate(POOL_WINDOWS):
        ls = slice(g * POOL_GC, (g + 1) * POOL_GC)
        acc = ubuf_ref[:, ls]
        lo, step = 0, 1
        while step < w:
            acc = acc[step:] + acc[:-step]
            lo += step
            step *= 2
        acc = acc[POOL_HALO - lo:]
        inv_cnt = 1.0 / jnp.minimum(pos + 1, w).astype(F32)
        pooled.append((acc * inv_cnt - u[:, ls]).astype(BF16))
    ubuf_ref[0:POOL_HALO, :] = ubuf_ref[tm:tm + POOL_HALO, :]
    mix = mix + _dot(jnp.concatenate(pooled, axis=1), wcomb_ref[...])

    h_ref[...] = x_ref[...] + mix

    qkg_ref[...] = qkg_next_ref[...]
    vg_ref[...] = vg_next_ref[...]
    u_ref[...] = u_next_ref[...]


def _ffn_kernel(h_ref, p_ref, lnf_ref, wg_ref, wu_ref, wd_ref, lnp_ref, wpg_ref, wpp_ref,
                lnfin_ref, out_ref, hid_ref):
    ple = _dot(p_ref[...].astype(BF16), wpp_ref[...])
    h = h_ref[...]
    f = _rmsnorm(h, lnf_ref[...]).astype(BF16)
    for c in range(0, D_FF, FF_CHUNK):
        g = _dot(f, wg_ref[:, c:c + FF_CHUNK])
        u = _dot(f, wu_ref[:, c:c + FF_CHUNK])
        hid_ref[:, c:c + FF_CHUNK] = (g * jax.nn.sigmoid(g) * u).astype(BF16)
    half = h.shape[0] // 2
    halves = (slice(0, half), slice(half, 2 * half))
    mid = [h[rows] + _dot(hid_ref[rows, :], wd_ref[...]) for rows in halves]
    gates = [jax.nn.sigmoid(_dot(_rmsnorm(x, lnp_ref[...]).astype(BF16), wpg_ref[...]))
             for x in mid]
    for rows, x, gate in zip(halves, mid, gates):
        out_ref[rows, :] = _rmsnorm(x + gate * ple[rows], lnfin_ref[...])


def _selection_matrix():
    r = np.arange(SUB_BLOCK * GLA_KEY)
    j, hd = r // GLA_KEY, r % GLA_KEY
    c = np.arange(GLA_KEY)
    same_head = (hd // GLA_DK)[:, None] == (c // GLA_CHUNK)[None, :]
    same_col = j[:, None] == (c % SUB_BLOCK)[None, :]
    return jnp.asarray(same_head & same_col, dtype=BF16)


def _slab_specs(rows, cols, n_steps):
    slab = next(s for s in range(BF16_TILE_ROWS, rows + 1, BF16_TILE_ROWS)
                if rows % s == 0 and s * n_steps >= rows)
    last = rows // slab - 1
    return (pl.BlockSpec((None, slab, cols), lambda n: (0, jnp.minimum(n, last), 0)),
            pl.BlockSpec((slab, cols), lambda n: (jnp.minimum(n, last), 0)))


def _chunk_tril():
    r = np.arange(TRIL_ROWS)
    same_chunk = (r // GLA_CHUNK)[:, None] == (r // GLA_CHUNK)[None, :]
    return jnp.asarray(same_chunk & (r[None, :] <= r[:, None]), dtype=BF16)


@jax.jit
def _forward(x, p, ln_mix, w_in, w_gate_lr, b_gate, gla_norm, w_pool, pool_scale, w_out,
             ln_ffn, w_ffn_gate, w_ffn_up, w_ffn_down, ln_ple, w_ple_gate, w_ple_proj,
             ln_final):
    bsz, seq, _ = x.shape
    n_tok = bsz * seq
    h = x.reshape(n_tok, D_MODEL)
    assert ln_mix.shape[0] == 1
    tm = TM_MIX
    n_tiles = n_tok // tm
    ffn_weights = (w_ffn_gate, w_ffn_up, w_ffn_down, w_ple_gate, w_ple_proj)
    slabs = [_slab_specs(w.shape[1], w.shape[2], n_tiles) for w in ffn_weights]
    h, *ffn_w16 = pl.pallas_call(
        functools.partial(_mix_kernel, tm=tm, tiles_per_seq=seq // tm),
        grid=(n_tiles,),
        in_specs=[
            pl.BlockSpec((tm, D_MODEL), lambda n: (jnp.minimum(n + 1, n_tiles - 1), 0)),
            pl.BlockSpec((tm, D_MODEL), lambda n: (n, 0)),
            _const_spec((1, D_MODEL)),
            _const_spec((None, w_in.shape[2], w_in.shape[1])),
            _const_spec((None, GLA_GATE_RANK, GLA_KEY)),
            _const_spec((1, GLA_KEY)),
            _const_spec((TRIL_ROWS, TRIL_ROWS)),
            _const_spec((SUB_BLOCK * GLA_KEY, GLA_KEY)),
            _const_spec((1, GLA_DV)),
            _const_spec((None, len(POOL_WINDOWS), POOL_GC, POOL_GC)),
            _const_spec((1, POOL_WIDTH)),
            _const_spec((None, D_MODEL, D_MODEL)),
        ] + [src for src, _ in slabs],
        out_specs=[pl.BlockSpec((tm, D_MODEL), lambda n: (n, 0))] + [dst for _, dst in slabs],
        out_shape=[jax.ShapeDtypeStruct((n_tok, D_MODEL), F32)]
        + [jax.ShapeDtypeStruct(w.shape[1:], BF16) for w in ffn_weights],
        scratch_shapes=[
            pltpu.VMEM((D_MODEL, 2 * GLA_KEY + 2 * GLA_VAL + POOL_WIDTH), BF16),
            pltpu.VMEM((D_MODEL, GLA_KEY), BF16),
            pltpu.VMEM((D_MODEL, D_MODEL), BF16),
            pltpu.VMEM((POOL_WIDTH, D_MODEL), BF16),
            pltpu.VMEM((3 * KEY_TILES, tm, LANES), F32),
            pltpu.VMEM((tm, 2 * GLA_VAL), BF16),
            pltpu.VMEM((tm, POOL_WIDTH), F32),
            pltpu.VMEM((3 * KEY_TILES, tm, LANES), F32),
            pltpu.VMEM((tm, 2 * GLA_VAL), BF16),
            pltpu.VMEM((tm, POOL_WIDTH), F32),
            pltpu.VMEM((GLA_DV, GLA_KEY), F32),
            pltpu.VMEM((POOL_HALO + tm, POOL_WIDTH), F32),
            pltpu.VMEM((KEY_TILES, tm, LANES), F32),
            pltpu.VMEM((tm, SUBLANES * GLA_KEY), BF16),
            pltpu.VMEM((tm // 2, SUBLANES * GLA_KEY), BF16),
            pltpu.VMEM((tm, GLA_KEY), F32),
            pltpu.VMEM((tm, GLA_VAL), F32),
        ],
        compiler_params=pltpu.CompilerParams(
            dimension_semantics=("arbitrary",), vmem_limit_bytes=VMEM_LIMIT),
        name="mix",
    )(h, h, ln_mix, jnp.swapaxes(w_in, 1, 2), w_gate_lr, b_gate, _chunk_tril(),
      _selection_matrix(), gla_norm, w_pool, pool_scale, w_out, *ffn_weights)

    h = pl.pallas_call(
        _ffn_kernel,
        grid=(n_tok // TM_FFN,),
        in_specs=[
            pl.BlockSpec((TM_FFN, D_MODEL), lambda r: (r, 0)),
            pl.BlockSpec((TM_FFN, D_PLE), lambda r: (r, 0)),
            _const_spec((1, D_MODEL)),
            _const_spec((D_MODEL, D_FF)),
            _const_spec((D_MODEL, D_FF)),
            _const_spec((D_FF, D_MODEL)),
            _const_spec((1, D_MODEL)),
            _const_spec((D_MODEL, D_MODEL)),
            _const_spec((D_PLE, D_MODEL)),
            _const_spec((1, D_MODEL)),
        ],
        out_specs=pl.BlockSpec((TM_FFN, D_MODEL), lambda r: (r, 0)),
        out_shape=jax.ShapeDtypeStruct((n_tok, D_MODEL), F32),
        scratch_shapes=[pltpu.VMEM((TM_FFN, D_FF), BF16)],
        compiler_params=pltpu.CompilerParams(
            dimension_semantics=("arbitrary",), vmem_limit_bytes=VMEM_LIMIT),
        name="ffn",
    )(h, p.reshape(n_tok, D_PLE), ln_ffn, ffn_w16[0], ffn_w16[1], ffn_w16[2], ln_ple,
      ffn_w16[3], ffn_w16[4], ln_final[None])
    return h.reshape(bsz, seq, D_MODEL)


def kernel(x, p, ln_mix, w_in, w_gate_lr, b_gate, gla_norm, w_pool, pool_scale, w_out, ln_ffn,
           w_ffn_gate, w_ffn_up, w_ffn_down, ln_ple, w_ple_gate, w_ple_proj, ln_final):
    return _forward(x, p, ln_mix, w_in, w_gate_lr, b_gate, gla_norm, w_pool, pool_scale, w_out,
                    ln_ffn, w_ffn_gate, w_ffn_up, w_ffn_down, ln_ple, w_ple_gate, w_ple_proj,
                    ln_final)
```

```python
import functools
import math

import jax
import jax.numpy as jnp
import numpy as np
from jax import lax
from jax.experimental import pallas as pl
from jax.experimental.pallas import tpu as pltpu

D_MODEL = 1024
D_PLE = 256
GLA_HEADS = 4
GLA_DK = 64
GLA_DV = 128
GLA_KEY = GLA_HEADS * GLA_DK
GLA_VAL = GLA_HEADS * GLA_DV
GLA_GATE_RANK = 16
GLA_GATE_NORM = 16.0
GLA_CHUNK = 64
POOL_WINDOWS = (2, 4, 8, 16)
POOL_GC = 128
POOL_WIDTH = 512
D_FF = 2816
EPS = 1e-6

LANES = 128
SUBLANES = 8
BF16_TILE_ROWS = 16
KEY_TILES = GLA_KEY // LANES
HEADS_PER_TILE = LANES // GLA_DK
SUB_BLOCK = 16
N_SUB = GLA_CHUNK // SUB_BLOCK
N_CAT = SUB_BLOCK * (N_SUB * (N_SUB - 1) // 2)
TRIL_ROWS = 128
PROJ_COLS = 256
PROJECT_AFTER_PAIR = (1, 4, 7, 11)
PROJECT_AFTER_SCORES = (3, 7)
PROJECT_AFTER_OUTPUTS = (3,)
POOL_HALO = max(POOL_WINDOWS)
FF_CHUNK = 256
FFN_LATE_ROWS = 256
TM_MIX = 512
TM_FFN = 1024
VMEM_LIMIT = 56 * 1024 * 1024
LOG2E = math.log2(math.e)

F32 = jnp.float32
BF16 = jnp.bfloat16
NT = (((1,), (1,)), ((), ()))
TN = (((0,), (0,)), ((), ()))


def _dot(a, b, dims=None):
    if dims is None:
        return jnp.dot(a, b, preferred_element_type=F32)
    return lax.dot_general(a, b, dims, preferred_element_type=F32)


def _rmsnorm(x, g):
    return x * lax.rsqrt(jnp.mean(x * x, axis=-1, keepdims=True) + EPS) * g


def _row_bcast(x, row, n):
    return jnp.broadcast_to(x[row:row + 1, :], (n, x.shape[1]))


def _const_spec(shape):
    nd = len(shape)
    return pl.BlockSpec(shape, lambda *_: (0,) * nd, pipeline_mode=pl.Buffered(1))


def _projection_steps(x_ref, ln_ref, wmain_ref, wgfold_ref, bgate_ref,
                      qkg_out, vg_out, u_out):
    hold = {}

    def norm():
        hold["a"] = _rmsnorm(x_ref[...], ln_ref[...]).astype(BF16)

    def piece(col):
        def run():
            z = _dot(hold["a"], wmain_ref[:, col:col + PROJ_COLS])
            for i in range(PROJ_COLS // LANES):
                zi = z[:, i * LANES:(i + 1) * LANES]
                c = col + i * LANES
                if c < GLA_KEY:
                    qkg_out[c // LANES] = zi * (GLA_DK ** -0.5)
                elif c < 2 * GLA_KEY:
                    qkg_out[c // LANES] = zi
                elif c < 2 * GLA_KEY + 2 * GLA_VAL:
                    vg_out[:, c - 2 * GLA_KEY:c - 2 * GLA_KEY + LANES] = zi.astype(BF16)
                else:
                    c -= 2 * GLA_KEY + 2 * GLA_VAL
                    u_out[:, c:c + LANES] = zi
        return run

    def gate():
        pre = _dot(hold["a"], wgfold_ref[...]) + bgate_ref[...]
        gk = (jnp.minimum(pre, 0.0) - jnp.log1p(jnp.exp(-jnp.abs(pre)))) * (LOG2E / GLA_GATE_NORM)
        for i in range(KEY_TILES):
            qkg_out[2 * KEY_TILES + i] = gk[:, i * LANES:(i + 1) * LANES]

    n_cols = 2 * GLA_KEY + 2 * GLA_VAL + POOL_WIDTH
    return [norm] + [piece(c) for c in range(0, n_cols, PROJ_COLS)] + [gate]


def _mix_kernel(x_next_ref, x_ref, ln_ref, win_t_ref, wgatelr_ref, bgate_ref,
                tril_ref, esel_ref, gnorm_ref, wpool_ref, pscale_ref, wout32_ref,
                wg32_ref, wu32_ref, wd32_ref, wpg32_ref, wpp32_ref,
                h_ref, wg16_ref, wu16_ref, wd16_ref, wpg16_ref, wpp16_ref,
                wmain_ref, wgfold_ref, wout_ref, wcomb_ref,
                qkg_ref, vg_ref, u_ref, qkg_next_ref, vg_next_ref, u_next_ref, state_ref,
                ubuf_ref, b_ref, w1_ref, w2_ref, qoff_ref, o_ref, *, tm, tiles_per_seq):
    for src, dst in ((wg32_ref, wg16_ref), (wu32_ref, wu16_ref), (wd32_ref, wd16_ref),
                     (wpg32_ref, wpg16_ref), (wpp32_ref, wpp16_ref)):
        dst[...] = src[...].astype(BF16)

    n = pl.program_id(0)
    t = n % tiles_per_seq
    n_chunks = tm // GLA_CHUNK
    weights = (ln_ref, wmain_ref, wgfold_ref, bgate_ref)

    @pl.when(n == 0)
    def _():
        s_glr = 2 * GLA_KEY + 2 * GLA_VAL
        for c in range(0, s_glr + POOL_WIDTH, PROJ_COLS):
            r = c if c < s_glr else c + GLA_GATE_RANK
            wmain_ref[:, c:c + PROJ_COLS] = win_t_ref[r:r + PROJ_COLS, :].T.astype(BF16)
        g_lr_t = win_t_ref[s_glr:s_glr + LANES, :].T
        lane = lax.broadcasted_iota(jnp.int32, g_lr_t.shape, 1)
        w_glr = jnp.where(lane < GLA_GATE_RANK, g_lr_t, 0.0).astype(BF16)
        w_gate = jnp.concatenate(
            [wgatelr_ref[...].astype(BF16),
             jnp.zeros((LANES - GLA_GATE_RANK, GLA_KEY), BF16)], axis=0)
        wgfold_ref[...] = _dot(w_glr, w_gate).astype(BF16)
        wout_ref[...] = wout32_ref[...].astype(BF16)
        for step in _projection_steps(x_ref, *weights, qkg_ref, vg_ref, u_ref):
            step()
        for g in range(len(POOL_WINDOWS)):
            rows = slice(g * POOL_GC, (g + 1) * POOL_GC)
            scaled = (wpool_ref[g] * pscale_ref[:, rows]).astype(BF16)
            wcomb_ref[rows, :] = _dot(
                scaled, wout_ref[GLA_VAL + g * POOL_GC:GLA_VAL + (g + 1) * POOL_GC, :]).astype(BF16)

    @pl.when(t == 0)
    def _():
        state_ref[...] = jnp.zeros_like(state_ref)
        ubuf_ref[0:POOL_HALO, :] = jnp.zeros((POOL_HALO, POOL_WIDTH), F32)

    pending = _projection_steps(x_next_ref, *weights, qkg_next_ref, vg_next_ref, u_next_ref)

    def project_next(k):
        for _ in range(min(k, len(pending))):
            pending.pop(0)()

    vg = vg_ref
    project_next(1)

    def q_tile(i):
        return qkg_ref.at[i]

    def k_tile(i):
        return qkg_ref.at[KEY_TILES + i]

    gk = jnp.concatenate([qkg_ref[2 * KEY_TILES + i] for i in range(KEY_TILES)], axis=1)
    g_hi = gk.astype(BF16)
    g_lo = (gk - g_hi.astype(F32)).astype(BF16)
    tril = tril_ref[...]
    for r0 in range(0, tm, TRIL_ROWS):
        rs = slice(r0, r0 + TRIL_ROWS)
        b_blk = _dot(tril, g_hi[rs]) + _dot(tril, g_lo[rs])
        for i in range(KEY_TILES):
            b_ref[i, rs, :] = b_blk[:, i * LANES:(i + 1) * LANES]

    row8 = lax.broadcasted_iota(jnp.int32, (SUBLANES, LANES), 0)
    neg_inf = jnp.full((SUBLANES, LANES), -jnp.inf, F32)

    def pair_body(p, first_pass):
        half = tm // 2
        for i in range(KEY_TILES):
            ld = lambda ref, g, hi: ref[pl.ds(g * SUB_BLOCK + hi * SUBLANES, SUBLANES), :]
            groups = (2 * p, 2 * p + 1)
            src = 0 if first_pass else 1
            b_hi = [ld(b_ref.at[i], g, 1) for g in groups]
            q_hi = [ld(q_tile(i), g, 1) for g in groups]
            b_src = [ld(b_ref.at[i], g, src) for g in groups]
            k_src = [ld(k_tile(i), g, src) for g in groups]
            if first_pass:
                q_lo = [ld(q_tile(i), g, 0) for g in groups]
            for s in range(SUBLANES):
                w_lo, w_hi = [], []
                for n_g, g in enumerate(groups):
                    b_j = _row_bcast(b_src[n_g], s, SUBLANES)
                    k_j = _row_bcast(k_src[n_g], s, SUBLANES)
                    if first_pass:
                        e_lo = jnp.exp2(jnp.where(row8 >= s, b_src[n_g] - b_j, neg_inf))
                        e_hi = jnp.exp2(b_hi[n_g] - b_j)
                        w_lo.append(q_lo[n_g] * k_j * e_lo)
                    else:
                        e_hi = jnp.exp2(jnp.where(row8 >= s, b_hi[n_g] - b_j, neg_inf))
                    w_hi.append(q_hi[n_g] * k_j * e_hi)
                    if first_pass and s == 0:
                        qoff_ref[pl.ds(g * SUB_BLOCK, SUB_BLOCK), i * LANES:(i + 1) * LANES] = (
                            jnp.concatenate([q_lo[n_g] * e_lo, q_hi[n_g] * e_hi], axis=0))
                col = s * GLA_KEY + i * LANES
                w_hi = jnp.concatenate(w_hi, axis=0).astype(BF16)
                if first_pass:
                    w1_ref[pl.ds(p * SUB_BLOCK, SUB_BLOCK), col:col + LANES] = (
                        jnp.concatenate(w_lo, axis=0).astype(BF16))
                    w1_ref[pl.ds(half + p * SUB_BLOCK, SUB_BLOCK), col:col + LANES] = w_hi
                else:
                    w2_ref[pl.ds(p * SUB_BLOCK, SUB_BLOCK), col:col + LANES] = w_hi

    n_grp = tm // SUB_BLOCK
    half_k = SUBLANES * GLA_KEY
    for p in range(n_grp // 2):
        pair_body(p, True)
        if p in PROJECT_AFTER_PAIR:
            project_next(1)
    a_first = _dot(w1_ref[...], esel_ref[0:half_k, :])
    for p in range(n_grp // 2):
        pair_body(p, False)
    a_lo = a_first[0:tm // 2]
    a_hi = a_first[tm // 2:] + _dot(w2_ref[...], esel_ref[half_k:, :])
    a_diag = jnp.concatenate(
        [a_lo.reshape(n_grp, SUBLANES, GLA_KEY), a_hi.reshape(n_grp, SUBLANES, GLA_KEY)],
        axis=1).reshape(tm, GLA_KEY)

    lane_head = (lax.broadcasted_iota(jnp.int32, (tm, GLA_KEY), 1) % LANES) // GLA_DK
    blk_row = (lax.broadcasted_iota(jnp.int32, (tm, GLA_KEY), 0) % GLA_CHUNK) // SUB_BLOCK
    blk_lane = (lax.broadcasted_iota(jnp.int32, (tm, GLA_KEY), 1) % GLA_CHUNK) // SUB_BLOCK
    b_all = jnp.concatenate([b_ref[i] for i in range(KEY_TILES)], axis=1)
    q_all = jnp.concatenate([q_tile(i)[...] for i in range(KEY_TILES)], axis=1)
    q_in = (q_all * jnp.exp2(b_all)).astype(BF16)
    a_dg = jnp.where(blk_row == blk_lane, a_diag, 0.0).astype(BF16)
    q_off_all = qoff_ref[...]
    q_off = [jnp.where(lane_head == hh, q_off_all, 0.0).astype(BF16)
             for hh in range(HEADS_PER_TILE)]

    stacked = HEADS_PER_TILE * GLA_CHUNK
    prow = (lax.broadcasted_iota(jnp.int32, (stacked, N_CAT), 0) % GLA_CHUNK) // SUB_BLOCK
    pcol = lax.broadcasted_iota(jnp.int32, (stacked, N_CAT), 1)
    pgrp = jnp.zeros_like(pcol)
    start = 0
    for i in range(1, N_SUB):
        pgrp = jnp.where((pcol >= start) & (pcol < start + i * SUB_BLOCK), i, pgrp)
        start += i * SUB_BLOCK
    off_mask = prow == pgrp
    first_head_lanes = lax.broadcasted_iota(jnp.int32, (GLA_DV, LANES), 1) < GLA_DK
    first_head_rows = lax.broadcasted_iota(jnp.int32, (LANES, GLA_DV), 0) < GLA_DK
    zeros_v = jnp.zeros((GLA_CHUNK, GLA_DV), BF16)

    p_off = {}
    d_state = {}
    decay = {}
    for c in range(n_chunks):
        r0 = c * GLA_CHUNK
        rs = slice(r0, r0 + GLA_CHUNK)
        for i in range(KEY_TILES):
            ls = slice(i * LANES, (i + 1) * LANES)
            bc = b_ref[i, rs, :]
            kc = k_tile(i)[rs, :]
            b_last = _row_bcast(bc, GLA_CHUNK - 1, GLA_CHUNK)
            k_dec = (kc * jnp.exp2(b_last - bc)).astype(BF16)
            decay[c, i] = jnp.exp2(b_last[0:SUBLANES])
            pieces = []
            for s in range(1, N_SUB):
                m = s * SUB_BLOCK
                r_s = _row_bcast(bc, m, m)
                pieces.append(kc[0:m] * jnp.exp2(r_s - bc[0:m]))
            k_cat = jnp.concatenate(pieces, axis=0).astype(BF16)
            q_two = jnp.concatenate([q[rs, ls] for q in q_off], axis=0)
            p_off[c, i] = jnp.where(off_mask, _dot(q_two, k_cat, NT), 0.0).astype(BF16)
            ds = _dot(vg[rs, i * HEADS_PER_TILE * GLA_DV:(i + 1) * HEADS_PER_TILE * GLA_DV],
                      k_dec, TN)
            d_state[c, i] = jnp.where(first_head_lanes, ds[0:GLA_DV], ds[GLA_DV:])
        if c in PROJECT_AFTER_SCORES:
            project_next(1)

    state_in = {}
    for i in range(KEY_TILES):
        ls = slice(i * LANES, (i + 1) * LANES)
        s = state_ref[:, ls]
        for c in range(n_chunks):
            s_t = s.T
            state_in[c, i] = jnp.concatenate(
                [jnp.where(first_head_rows, s_t, 0.0), jnp.where(first_head_rows, 0.0, s_t)],
                axis=1).astype(BF16)
            s = (s.reshape(GLA_DV // SUBLANES, SUBLANES, LANES) * decay[c, i][None]
                 ).reshape(GLA_DV, LANES) + d_state[c, i]
        state_ref[:, ls] = s

    for c in range(n_chunks):
        r0 = c * GLA_CHUNK
        rs = slice(r0, r0 + GLA_CHUNK)
        for i in range(KEY_TILES):
            ls = slice(i * LANES, (i + 1) * LANES)
            vs = slice(i * HEADS_PER_TILE * GLA_DV, (i + 1) * HEADS_PER_TILE * GLA_DV)
            v_heads = [vg[rs, (i * HEADS_PER_TILE + hh) * GLA_DV:(i * HEADS_PER_TILE + hh + 1) * GLA_DV]
                       for hh in range(HEADS_PER_TILE)]
            off = []
            for hh, vc in enumerate(v_heads):
                v_cat = jnp.concatenate([vc[0:s * SUB_BLOCK] for s in range(1, N_SUB)], axis=0)
                off.append(_dot(p_off[c, i][hh * GLA_CHUNK:(hh + 1) * GLA_CHUNK], v_cat))
            v_diag = jnp.concatenate(
                [jnp.concatenate([v_heads[0], zeros_v], axis=1),
                 jnp.concatenate([zeros_v, v_heads[1]], axis=1)], axis=0)
            o_ref[rs, vs] = (jnp.concatenate(off, axis=1) + _dot(a_dg[rs, ls], v_diag)
                             + _dot(q_in[rs, ls], state_in[c, i]))
        if c in PROJECT_AFTER_OUTPUTS:
            project_next(1)

    gated = []
    for h in range(GLA_HEADS):
        vs = slice(h * GLA_DV, (h + 1) * GLA_DV)
        o_h = _rmsnorm(o_ref[:, vs], gnorm_ref[...])
        g_h = vg[:, GLA_VAL + h * GLA_DV:GLA_VAL + (h + 1) * GLA_DV].astype(F32)
        gated.append((o_h * (g_h * jax.nn.sigmoid(g_h))).astype(BF16))
    project_next(len(pending))
    mix = _dot(jnp.concatenate(gated, axis=1), wout_ref[0:GLA_VAL, :])

    u = u_ref[...]
    ubuf_ref[POOL_HALO:POOL_HALO + tm, :] = u
    pos = t * tm + lax.broadcasted_iota(jnp.int32, (tm, 1), 0)
    pooled = []
    for g, w in enumerate(POOL_WINDOWS):
        ls = slice(g * POOL_GC, (g + 1) * POOL_GC)
        acc = ubuf_ref[:, ls]
        lo, step = 0, 1
        while step < w:
            acc = acc[step:] + acc[:-step]
            lo += step
            step *= 2
        acc = acc[POOL_HALO - lo:]
        inv_cnt = 1.0 / jnp.minimum(pos + 1, w).astype(F32)
        pooled.append((acc * inv_cnt - u[:, ls]).astype(BF16))
    ubuf_ref[0:POOL_HALO, :] = ubuf_ref[tm:tm + POOL_HALO, :]
    mix = mix + _dot(jnp.concatenate(pooled, axis=1), wcomb_ref[...])

    h_ref[...] = x_ref[...] + mix

    qkg_ref[...] = qkg_next_ref[...]
    vg_ref[...] = vg_next_ref[...]
    u_ref[...] = u_next_ref[...]


def _ffn_kernel(h_ref, p_ref, lnf_ref, wg_ref, wu_ref, wd_ref, lnp_ref, wpg_ref, wpp_ref,
                lnfin_ref, out_ref, hid_ref):
    ple = _dot(p_ref[...].astype(BF16), wpp_ref[...])
    h = h_ref[...]
    f = _rmsnorm(h, lnf_ref[...]).astype(BF16)
    for c in range(0, D_FF, FF_CHUNK):
        g = _dot(f, wg_ref[:, c:c + FF_CHUNK])
        u = _dot(f, wu_ref[:, c:c + FF_CHUNK])
        hid_ref[:, c:c + FF_CHUNK] = (g * jax.nn.sigmoid(g) * u).astype(BF16)
    blocks = tuple(slice(r, r + FFN_LATE_ROWS) for r in range(0, h.shape[0], FFN_LATE_ROWS))
    mid = [h[rows] + _dot(hid_ref[rows, :], wd_ref[...]) for rows in blocks]
    gates = [jax.nn.sigmoid(_dot(_rmsnorm(x, lnp_ref[...]).astype(BF16), wpg_ref[...]))
             for x in mid]
    for rows, x, gate in zip(blocks, mid, gates):
        out_ref[rows, :] = _rmsnorm(x + gate * ple[rows], lnfin_ref[...])


def _selection_matrix():
    r = np.arange(SUB_BLOCK * GLA_KEY)
    j, hd = r // GLA_KEY, r % GLA_KEY
    c = np.arange(GLA_KEY)
    same_head = (hd // GLA_DK)[:, None] == (c // GLA_CHUNK)[None, :]
    same_col = j[:, None] == (c % SUB_BLOCK)[None, :]
    return jnp.asarray(same_head & same_col, dtype=BF16)


def _slab_specs(rows, cols, n_steps):
    slab = next(s for s in range(BF16_TILE_ROWS, rows + 1, BF16_TILE_ROWS)
                if rows % s == 0 and s * n_steps >= rows)
    last = rows // slab - 1
    return (pl.BlockSpec((None, slab, cols), lambda n: (0, jnp.minimum(n, last), 0)),
            pl.BlockSpec((slab, cols), lambda n: (jnp.minimum(n, last), 0)))


def _chunk_tril():
    r = np.arange(TRIL_ROWS)
    same_chunk = (r // GLA_CHUNK)[:, None] == (r // GLA_CHUNK)[None, :]
    return jnp.asarray(same_chunk & (r[None, :] <= r[:, None]), dtype=BF16)


@jax.jit
def _forward(x, p, ln_mix, w_in, w_gate_lr, b_gate, gla_norm, w_pool, pool_scale, w_out,
             ln_ffn, w_ffn_gate, w_ffn_up, w_ffn_down, ln_ple, w_ple_gate, w_ple_proj,
             ln_final):
    bsz, seq, _ = x.shape
    n_tok = bsz * seq
    h = x.reshape(n_tok, D_MODEL)
    assert ln_mix.shape[0] == 1
    tm = TM_MIX
    n_tiles = n_tok // tm
    ffn_weights = (w_ffn_gate, w_ffn_up, w_ffn_down, w_ple_gate, w_ple_proj)
    slabs = [_slab_specs(w.shape[1], w.shape[2], n_tiles) for w in ffn_weights]
    h, *ffn_w16 = pl.pallas_call(
        functools.partial(_mix_kernel, tm=tm, tiles_per_seq=seq // tm),
        grid=(n_tiles,),
        in_specs=[
            pl.BlockSpec((tm, D_MODEL), lambda n: (jnp.minimum(n + 1, n_tiles - 1), 0)),
            pl.BlockSpec((tm, D_MODEL), lambda n: (n, 0)),
            _const_spec((1, D_MODEL)),
            _const_spec((None, w_in.shape[2], w_in.shape[1])),
            _const_spec((None, GLA_GATE_RANK, GLA_KEY)),
            _const_spec((1, GLA_KEY)),
            _const_spec((TRIL_ROWS, TRIL_ROWS)),
            _const_spec((SUB_BLOCK * GLA_KEY, GLA_KEY)),
            _const_spec((1, GLA_DV)),
            _const_spec((None, len(POOL_WINDOWS), POOL_GC, POOL_GC)),
            _const_spec((1, POOL_WIDTH)),
            _const_spec((None, D_MODEL, D_MODEL)),
        ] + [src for src, _ in slabs],
        out_specs=[pl.BlockSpec((tm, D_MODEL), lambda n: (n, 0))] + [dst for _, dst in slabs],
        out_shape=[jax.ShapeDtypeStruct((n_tok, D_MODEL), F32)]
        + [jax.ShapeDtypeStruct(w.shape[1:], BF16) for w in ffn_weights],
        scratch_shapes=[
            pltpu.VMEM((D_MODEL, 2 * GLA_KEY + 2 * GLA_VAL + POOL_WIDTH), BF16),
            pltpu.VMEM((D_MODEL, GLA_KEY), BF16),
            pltpu.VMEM((D_MODEL, D_MODEL), BF16),
            pltpu.VMEM((POOL_WIDTH, D_MODEL), BF16),
            pltpu.VMEM((3 * KEY_TILES, tm, LANES), F32),
            pltpu.VMEM((tm, 2 * GLA_VAL), BF16),
            pltpu.VMEM((tm, POOL_WIDTH), F32),
            pltpu.VMEM((3 * KEY_TILES, tm, LANES), F32),
            pltpu.VMEM((tm, 2 * GLA_VAL), BF16),
            pltpu.VMEM((tm, POOL_WIDTH), F32),
            pltpu.VMEM((GLA_DV, GLA_KEY), F32),
            pltpu.VMEM((POOL_HALO + tm, POOL_WIDTH), F32),
            pltpu.VMEM((KEY_TILES, tm, LANES), F32),
            pltpu.VMEM((tm, SUBLANES * GLA_KEY), BF16),
            pltpu.VMEM((tm // 2, SUBLANES * GLA_KEY), BF16),
            pltpu.VMEM((tm, GLA_KEY), F32),
            pltpu.VMEM((tm, GLA_VAL), F32),
        ],
        compiler_params=pltpu.CompilerParams(
            dimension_semantics=("arbitrary",), vmem_limit_bytes=VMEM_LIMIT),
        name="mix",
    )(h, h, ln_mix, jnp.swapaxes(w_in, 1, 2), w_gate_lr, b_gate, _chunk_tril(),
      _selection_matrix(), gla_norm, w_pool, pool_scale, w_out, *ffn_weights)

    h = pl.pallas_call(
        _ffn_kernel,
        grid=(n_tok // TM_FFN,),
        in_specs=[
            pl.BlockSpec((TM_FFN, D_MODEL), lambda r: (r, 0)),
            pl.BlockSpec((TM_FFN, D_PLE), lambda r: (r, 0)),
            _const_spec((1, D_MODEL)),
            _const_spec((D_MODEL, D_FF)),
            _const_spec((D_MODEL, D_FF)),
            _const_spec((D_FF, D_MODEL)),
            _const_spec((1, D_MODEL)),
            _const_spec((D_MODEL, D_MODEL)),
            _const_spec((D_PLE, D_MODEL)),
            _const_spec((1, D_MODEL)),
        ],
        out_specs=pl.BlockSpec((TM_FFN, D_MODEL), lambda r: (r, 0)),
        out_shape=jax.ShapeDtypeStruct((n_tok, D_MODEL), F32),
        scratch_shapes=[pltpu.VMEM((TM_FFN, D_FF), BF16)],
        compiler_params=pltpu.CompilerParams(
            dimension_semantics=("arbitrary",), vmem_limit_bytes=VMEM_LIMIT),
        name="ffn",
    )(h, p.reshape(n_tok, D_PLE), ln_ffn, ffn_w16[0], ffn_w16[1], ffn_w16[2], ln_ple,
      ffn_w16[3], ffn_w16[4], ln_final[None])
    return h.reshape(bsz, seq, D_MODEL)


def kernel(x, p, ln_mix, w_in, w_gate_lr, b_gate, gla_norm, w_pool, pool_scale, w_out, ln_ffn,
           w_ffn_gate, w_ffn_up, w_ffn_down, ln_ple, w_ple_gate, w_ple_proj, ln_final):
    return _forward(x, p, ln_mix, w_in, w_gate_lr, b_gate, gla_norm, w_pool, pool_scale, w_out,
                    ln_ffn, w_ffn_gate, w_ffn_up, w_ffn_down, ln_ple, w_ple_gate, w_ple_proj,
                    ln_final)
```

```python
import functools
import math

import jax
import jax.numpy as jnp
import numpy as np
from jax import lax
from jax.experimental import pallas as pl
from jax.experimental.pallas import tpu as pltpu

D_MODEL = 1024
D_PLE = 256
GLA_HEADS = 4
GLA_DK = 64
GLA_DV = 128
GLA_KEY = GLA_HEADS * GLA_DK
GLA_VAL = GLA_HEADS * GLA_DV
GLA_GATE_RANK = 16
GLA_GATE_NORM = 16.0
GLA_CHUNK = 64
POOL_WINDOWS = (2, 4, 8, 16)
POOL_GC = 128
POOL_WIDTH = 512
D_FF = 2816
EPS = 1e-6

LANES = 128
SUBLANES = 8
BF16_TILE_ROWS = 16
KEY_TILES = GLA_KEY // LANES
HEADS_PER_TILE = LANES // GLA_DK
SUB_BLOCK = 16
N_SUB = GLA_CHUNK // SUB_BLOCK
N_CAT = SUB_BLOCK * (N_SUB * (N_SUB - 1) // 2)
TRIL_ROWS = 128
PROJ_COLS = 512
WEIGHT_ROWS = 256
PROJECT_AFTER_PAIR = (5,)
PROJECT_AFTER_SCORES = (3, 7)
PROJECT_AFTER_OUTPUTS = (3,)
POOL_HALO = max(POOL_WINDOWS)
FF_CHUNK = 256
FFN_LATE_ROWS = 256
MIX_OUT_ROWS = 256
TM_MIX = 512
TM_FFN = 1024
VMEM_LIMIT = 56 * 1024 * 1024
LOG2E = math.log2(math.e)

F32 = jnp.float32
BF16 = jnp.bfloat16
NT = (((1,), (1,)), ((), ()))
TN = (((0,), (0,)), ((), ()))


def _dot(a, b, dims=None):
    if dims is None:
        return jnp.dot(a, b, preferred_element_type=F32)
    return lax.dot_general(a, b, dims, preferred_element_type=F32)


def _rmsnorm(x, g):
    return x * lax.rsqrt(jnp.mean(x * x, axis=-1, keepdims=True) + EPS) * g


def _row_bcast(x, row, n):
    return jnp.broadcast_to(x[row:row + 1, :], (n, x.shape[1]))


def _const_spec(shape):
    nd = len(shape)
    return pl.BlockSpec(shape, lambda *_: (0,) * nd, pipeline_mode=pl.Buffered(1))


def _projection_steps(x_ref, ln_ref, wmain_ref, wgfold_ref, bgate_ref,
                      qkg_out, vg_out, u_out):
    hold = {}

    def norm():
        hold["a"] = _rmsnorm(x_ref[...], ln_ref[...]).astype(BF16)

    def piece(col):
        def run():
            z = _dot(hold["a"], wmain_ref[:, col:col + PROJ_COLS])
            for i in range(PROJ_COLS // LANES):
                zi = z[:, i * LANES:(i + 1) * LANES]
                c = col + i * LANES
                if c < GLA_KEY:
                    qkg_out[c // LANES] = zi * (GLA_DK ** -0.5)
                elif c < 2 * GLA_KEY:
                    qkg_out[c // LANES] = zi
                elif c < 2 * GLA_KEY + 2 * GLA_VAL:
                    vg_out[:, c - 2 * GLA_KEY:c - 2 * GLA_KEY + LANES] = zi.astype(BF16)
                else:
                    c -= 2 * GLA_KEY + 2 * GLA_VAL
                    u_out[:, c:c + LANES] = zi
        return run

    def gate():
        pre = _dot(hold["a"], wgfold_ref[...]) + bgate_ref[...]
        gk = (jnp.minimum(pre, 0.0) - jnp.log1p(jnp.exp(-jnp.abs(pre)))) * (LOG2E / GLA_GATE_NORM)
        for i in range(KEY_TILES):
            qkg_out[2 * KEY_TILES + i] = gk[:, i * LANES:(i + 1) * LANES]

    n_cols = 2 * GLA_KEY + 2 * GLA_VAL + POOL_WIDTH
    return [norm] + [piece(c) for c in range(0, n_cols, PROJ_COLS)] + [gate]


def _mix_kernel(x_next_ref, x_ref, ln_ref, win_t_ref, wgatelr_ref, bgate_ref,
                tril_ref, esel_ref, gnorm_ref, wpool_ref, pscale_ref, wout32_ref,
                wg32_ref, wu32_ref, wd32_ref, wpg32_ref, wpp32_ref,
                h_ref, wg16_ref, wu16_ref, wd16_ref, wpg16_ref, wpp16_ref,
                wmain_ref, wgfold_ref, wout_ref, wcomb_ref,
                qkg_ref, vg_ref, u_ref, qkg_next_ref, vg_next_ref, u_next_ref, state_ref,
                ubuf_ref, b_ref, w1_ref, w2_ref, qoff_ref, o_ref, *, tm, tiles_per_seq):
    for src, dst in ((wg32_ref, wg16_ref), (wu32_ref, wu16_ref), (wd32_ref, wd16_ref),
                     (wpg32_ref, wpg16_ref), (wpp32_ref, wpp16_ref)):
        dst[...] = src[...].astype(BF16)

    n = pl.program_id(0)
    t = n % tiles_per_seq
    n_chunks = tm // GLA_CHUNK
    weights = (ln_ref, wmain_ref, wgfold_ref, bgate_ref)

    @pl.when(n == 0)
    def _():
        s_glr = 2 * GLA_KEY + 2 * GLA_VAL
        for c in range(0, s_glr + POOL_WIDTH, WEIGHT_ROWS):
            r = c if c < s_glr else c + GLA_GATE_RANK
            wmain_ref[:, c:c + WEIGHT_ROWS] = win_t_ref[r:r + WEIGHT_ROWS, :].T.astype(BF16)
        g_lr_t = win_t_ref[s_glr:s_glr + LANES, :].T
        lane = lax.broadcasted_iota(jnp.int32, g_lr_t.shape, 1)
        w_glr = jnp.where(lane < GLA_GATE_RANK, g_lr_t, 0.0).astype(BF16)
        w_gate = jnp.concatenate(
            [wgatelr_ref[...].astype(BF16),
             jnp.zeros((LANES - GLA_GATE_RANK, GLA_KEY), BF16)], axis=0)
        wgfold_ref[...] = _dot(w_glr, w_gate).astype(BF16)
        wout_ref[...] = wout32_ref[...].astype(BF16)
        for step in _projection_steps(x_ref, *weights, qkg_ref, vg_ref, u_ref):
            step()
        for g in range(len(POOL_WINDOWS)):
            rows = slice(g * POOL_GC, (g + 1) * POOL_GC)
            scaled = (wpool_ref[g] * pscale_ref[:, rows]).astype(BF16)
            wcomb_ref[rows, :] = _dot(
                scaled, wout_ref[GLA_VAL + g * POOL_GC:GLA_VAL + (g + 1) * POOL_GC, :]).astype(BF16)

    @pl.when(t == 0)
    def _():
        state_ref[...] = jnp.zeros_like(state_ref)
        ubuf_ref[0:POOL_HALO, :] = jnp.zeros((POOL_HALO, POOL_WIDTH), F32)

    pending = _projection_steps(x_next_ref, *weights, qkg_next_ref, vg_next_ref, u_next_ref)

    def project_next(k):
        for _ in range(min(k, len(pending))):
            pending.pop(0)()

    vg = vg_ref
    project_next(1)

    def q_tile(i):
        return qkg_ref.at[i]

    def k_tile(i):
        return qkg_ref.at[KEY_TILES + i]

    gk = jnp.concatenate([qkg_ref[2 * KEY_TILES + i] for i in range(KEY_TILES)], axis=1)
    g_hi = gk.astype(BF16)
    g_lo = (gk - g_hi.astype(F32)).astype(BF16)
    tril = tril_ref[...]
    for r0 in range(0, tm, TRIL_ROWS):
        rs = slice(r0, r0 + TRIL_ROWS)
        b_blk = _dot(tril, g_hi[rs]) + _dot(tril, g_lo[rs])
        for i in range(KEY_TILES):
            b_ref[i, rs, :] = b_blk[:, i * LANES:(i + 1) * LANES]

    row8 = lax.broadcasted_iota(jnp.int32, (SUBLANES, LANES), 0)
    neg_inf = jnp.full((SUBLANES, LANES), -jnp.inf, F32)

    def pair_body(p, first_pass):
        half = tm // 2
        for i in range(KEY_TILES):
            ld = lambda ref, g, hi: ref[pl.ds(g * SUB_BLOCK + hi * SUBLANES, SUBLANES), :]
            groups = (2 * p, 2 * p + 1)
            src = 0 if first_pass else 1
            b_hi = [ld(b_ref.at[i], g, 1) for g in groups]
            q_hi = [ld(q_tile(i), g, 1) for g in groups]
            b_src = [ld(b_ref.at[i], g, src) for g in groups]
            k_src = [ld(k_tile(i), g, src) for g in groups]
            if first_pass:
                q_lo = [ld(q_tile(i), g, 0) for g in groups]
            for s in range(SUBLANES):
                w_lo, w_hi = [], []
                for n_g, g in enumerate(groups):
                    b_j = _row_bcast(b_src[n_g], s, SUBLANES)
                    k_j = _row_bcast(k_src[n_g], s, SUBLANES)
                    if first_pass:
                        e_lo = jnp.exp2(jnp.where(row8 >= s, b_src[n_g] - b_j, neg_inf))
                        e_hi = jnp.exp2(b_hi[n_g] - b_j)
                        w_lo.append(q_lo[n_g] * k_j * e_lo)
                    else:
                        e_hi = jnp.exp2(jnp.where(row8 >= s, b_hi[n_g] - b_j, neg_inf))
                    w_hi.append(q_hi[n_g] * k_j * e_hi)
                    if first_pass and s == 0:
                        qoff_ref[pl.ds(g * SUB_BLOCK, SUB_BLOCK), i * LANES:(i + 1) * LANES] = (
                            jnp.concatenate([q_lo[n_g] * e_lo, q_hi[n_g] * e_hi], axis=0))
                col = s * GLA_KEY + i * LANES
                w_hi = jnp.concatenate(w_hi, axis=0).astype(BF16)
                if first_pass:
                    w1_ref[pl.ds(p * SUB_BLOCK, SUB_BLOCK), col:col + LANES] = (
                        jnp.concatenate(w_lo, axis=0).astype(BF16))
                    w1_ref[pl.ds(half + p * SUB_BLOCK, SUB_BLOCK), col:col + LANES] = w_hi
                else:
                    w2_ref[pl.ds(p * SUB_BLOCK, SUB_BLOCK), col:col + LANES] = w_hi

    n_grp = tm // SUB_BLOCK
    half_k = SUBLANES * GLA_KEY
    for p in range(n_grp // 2):
        pair_body(p, True)
        if p in PROJECT_AFTER_PAIR:
            project_next(1)
    a_first = _dot(w1_ref[...], esel_ref[0:half_k, :])
    for p in range(n_grp // 2):
        pair_body(p, False)
    a_lo = a_first[0:tm // 2]
    a_hi = a_first[tm // 2:] + _dot(w2_ref[...], esel_ref[half_k:, :])
    a_diag = jnp.concatenate(
        [a_lo.reshape(n_grp, SUBLANES, GLA_KEY), a_hi.reshape(n_grp, SUBLANES, GLA_KEY)],
        axis=1).reshape(tm, GLA_KEY)

    lane_head = (lax.broadcasted_iota(jnp.int32, (tm, GLA_KEY), 1) % LANES) // GLA_DK
    blk_row = (lax.broadcasted_iota(jnp.int32, (tm, GLA_KEY), 0) % GLA_CHUNK) // SUB_BLOCK
    blk_lane = (lax.broadcasted_iota(jnp.int32, (tm, GLA_KEY), 1) % GLA_CHUNK) // SUB_BLOCK
    b_all = jnp.concatenate([b_ref[i] for i in range(KEY_TILES)], axis=1)
    q_all = jnp.concatenate([q_tile(i)[...] for i in range(KEY_TILES)], axis=1)
    q_in = (q_all * jnp.exp2(b_all)).astype(BF16)
    a_dg = jnp.where(blk_row == blk_lane, a_diag, 0.0).astype(BF16)
    q_off_all = qoff_ref[...]
    q_off = [jnp.where(lane_head == hh, q_off_all, 0.0).astype(BF16)
             for hh in range(HEADS_PER_TILE)]

    stacked = HEADS_PER_TILE * GLA_CHUNK
    prow = (lax.broadcasted_iota(jnp.int32, (stacked, N_CAT), 0) % GLA_CHUNK) // SUB_BLOCK
    pcol = lax.broadcasted_iota(jnp.int32, (stacked, N_CAT), 1)
    pgrp = jnp.zeros_like(pcol)
    start = 0
    for i in range(1, N_SUB):
        pgrp = jnp.where((pcol >= start) & (pcol < start + i * SUB_BLOCK), i, pgrp)
        start += i * SUB_BLOCK
    off_mask = prow == pgrp
    first_head_lanes = lax.broadcasted_iota(jnp.int32, (GLA_DV, LANES), 1) < GLA_DK
    first_head_rows = lax.broadcasted_iota(jnp.int32, (LANES, GLA_DV), 0) < GLA_DK
    zeros_v = jnp.zeros((GLA_CHUNK, GLA_DV), BF16)

    p_off = {}
    d_state = {}
    decay = {}
    for c in range(n_chunks):
        r0 = c * GLA_CHUNK
        rs = slice(r0, r0 + GLA_CHUNK)
        for i in range(KEY_TILES):
            ls = slice(i * LANES, (i + 1) * LANES)
            bc = b_ref[i, rs, :]
            kc = k_tile(i)[rs, :]
            b_last = _row_bcast(bc, GLA_CHUNK - 1, GLA_CHUNK)
            k_dec = (kc * jnp.exp2(b_last - bc)).astype(BF16)
            decay[c, i] = jnp.exp2(b_last[0:SUBLANES])
            pieces = []
            for s in range(1, N_SUB):
                m = s * SUB_BLOCK
                r_s = _row_bcast(bc, m, m)
                pieces.append(kc[0:m] * jnp.exp2(r_s - bc[0:m]))
            k_cat = jnp.concatenate(pieces, axis=0).astype(BF16)
            q_two = jnp.concatenate([q[rs, ls] for q in q_off], axis=0)
            p_off[c, i] = jnp.where(off_mask, _dot(q_two, k_cat, NT), 0.0).astype(BF16)
            ds = _dot(vg[rs, i * HEADS_PER_TILE * GLA_DV:(i + 1) * HEADS_PER_TILE * GLA_DV],
                      k_dec, TN)
            d_state[c, i] = jnp.where(first_head_lanes, ds[0:GLA_DV], ds[GLA_DV:])
        if c in PROJECT_AFTER_SCORES:
            project_next(1)

    state_in = {}
    for i in range(KEY_TILES):
        ls = slice(i * LANES, (i + 1) * LANES)
        s = state_ref[:, ls]
        for c in range(n_chunks):
            s_t = s.T
            state_in[c, i] = jnp.concatenate(
                [jnp.where(first_head_rows, s_t, 0.0), jnp.where(first_head_rows, 0.0, s_t)],
                axis=1).astype(BF16)
            s = (s.reshape(GLA_DV // SUBLANES, SUBLANES, LANES) * decay[c, i][None]
                 ).reshape(GLA_DV, LANES) + d_state[c, i]
        state_ref[:, ls] = s

    for c in range(n_chunks):
        r0 = c * GLA_CHUNK
        rs = slice(r0, r0 + GLA_CHUNK)
        for i in range(KEY_TILES):
            ls = slice(i * LANES, (i + 1) * LANES)
            vs = slice(i * HEADS_PER_TILE * GLA_DV, (i + 1) * HEADS_PER_TILE * GLA_DV)
            v_heads = [vg[rs, (i * HEADS_PER_TILE + hh) * GLA_DV:(i * HEADS_PER_TILE + hh + 1) * GLA_DV]
                       for hh in range(HEADS_PER_TILE)]
            off = []
            for hh, vc in enumerate(v_heads):
                v_cat = jnp.concatenate([vc[0:s * SUB_BLOCK] for s in range(1, N_SUB)], axis=0)
                off.append(_dot(p_off[c, i][hh * GLA_CHUNK:(hh + 1) * GLA_CHUNK], v_cat))
            v_diag = jnp.concatenate(
                [jnp.concatenate([v_heads[0], zeros_v], axis=1),
                 jnp.concatenate([zeros_v, v_heads[1]], axis=1)], axis=0)
            o_ref[rs, vs] = (jnp.concatenate(off, axis=1) + _dot(a_dg[rs, ls], v_diag)
                             + _dot(q_in[rs, ls], state_in[c, i]))
        if c in PROJECT_AFTER_OUTPUTS:
            project_next(1)

    project_next(len(pending))

    u = u_ref[...]
    ubuf_ref[POOL_HALO:POOL_HALO + tm, :] = u
    pos = t * tm + lax.broadcasted_iota(jnp.int32, (tm, 1), 0)
    pooled = []
    for g, w in enumerate(POOL_WINDOWS):
        ls = slice(g * POOL_GC, (g + 1) * POOL_GC)
        acc = ubuf_ref[:, ls]
        lo, step = 0, 1
        while step < w:
            acc = acc[step:] + acc[:-step]
            lo += step
            step *= 2
        acc = acc[POOL_HALO - lo:]
        inv_cnt = 1.0 / jnp.minimum(pos + 1, w).astype(F32)
        pooled.append((acc * inv_cnt - u[:, ls]).astype(BF16))
    ubuf_ref[0:POOL_HALO, :] = ubuf_ref[tm:tm + POOL_HALO, :]
    pooled = jnp.concatenate(pooled, axis=1)

    for r0 in range(0, tm, MIX_OUT_ROWS):
        rows = slice(r0, r0 + MIX_OUT_ROWS)
        gated = []
        for h in range(GLA_HEADS):
            vs = slice(h * GLA_DV, (h + 1) * GLA_DV)
            o_h = _rmsnorm(o_ref[rows, vs], gnorm_ref[...])
            g_h = vg[rows, GLA_VAL + h * GLA_DV:GLA_VAL + (h + 1) * GLA_DV].astype(F32)
            gated.append((o_h * (g_h * jax.nn.sigmoid(g_h))).astype(BF16))
        h_ref[rows, :] = (x_ref[rows, :] + _dot(jnp.concatenate(gated, axis=1), wout_ref[0:GLA_VAL, :])
                          + _dot(pooled[rows], wcomb_ref[...]))

    qkg_ref[...] = qkg_next_ref[...]
    vg_ref[...] = vg_next_ref[...]
    u_ref[...] = u_next_ref[...]


def _ffn_kernel(h_ref, p_ref, lnf_ref, wg_ref, wu_ref, wd_ref, lnp_ref, wpg_ref, wpp_ref,
                lnfin_ref, out_ref, hid_ref):
    ple = _dot(p_ref[...].astype(BF16), wpp_ref[...])
    h = h_ref[...]
    f = _rmsnorm(h, lnf_ref[...]).astype(BF16)
    for c in range(0, D_FF, FF_CHUNK):
        g = _dot(f, wg_ref[:, c:c + FF_CHUNK])
        u = _dot(f, wu_ref[:, c:c + FF_CHUNK])
        hid_ref[:, c:c + FF_CHUNK] = (g * jax.nn.sigmoid(g) * u).astype(BF16)
    blocks = tuple(slice(r, r + FFN_LATE_ROWS) for r in range(0, h.shape[0], FFN_LATE_ROWS))
    mid = [h[rows] + _dot(hid_ref[rows, :], wd_ref[...]) for rows in blocks]
    gates = [jax.nn.sigmoid(_dot(_rmsnorm(x, lnp_ref[...]).astype(BF16), wpg_ref[...]))
             for x in mid]
    for rows, x, gate in zip(blocks, mid, gates):
        out_ref[rows, :] = _rmsnorm(x + gate * ple[rows], lnfin_ref[...])


def _selection_matrix():
    r = np.arange(SUB_BLOCK * GLA_KEY)
    j, hd = r // GLA_KEY, r % GLA_KEY
    c = np.arange(GLA_KEY)
    same_head = (hd // GLA_DK)[:, None] == (c // GLA_CHUNK)[None, :]
    same_col = j[:, None] == (c % SUB_BLOCK)[None, :]
    return jnp.asarray(same_head & same_col, dtype=BF16)


def _slab_specs(rows, cols, n_steps):
    slab = next(s for s in range(BF16_TILE_ROWS, rows + 1, BF16_TILE_ROWS)
                if rows % s == 0 and s * n_steps >= rows)
    last = rows // slab - 1
    return (pl.BlockSpec((None, slab, cols), lambda n: (0, jnp.minimum(n, last), 0)),
            pl.BlockSpec((slab, cols), lambda n: (jnp.minimum(n, last), 0)))


def _chunk_tril():
    r = np.arange(TRIL_ROWS)
    same_chunk = (r // GLA_CHUNK)[:, None] == (r // GLA_CHUNK)[None, :]
    return jnp.asarray(same_chunk & (r[None, :] <= r[:, None]), dtype=BF16)


@jax.jit
def _forward(x, p, ln_mix, w_in, w_gate_lr, b_gate, gla_norm, w_pool, pool_scale, w_out,
             ln_ffn, w_ffn_gate, w_ffn_up, w_ffn_down, ln_ple, w_ple_gate, w_ple_proj,
             ln_final):
    bsz, seq, _ = x.shape
    n_tok = bsz * seq
    h = x.reshape(n_tok, D_MODEL)
    assert ln_mix.shape[0] == 1
    tm = TM_MIX
    n_tiles = n_tok // tm
    ffn_weights = (w_ffn_gate, w_ffn_up, w_ffn_down, w_ple_gate, w_ple_proj)
    slabs = [_slab_specs(w.shape[1], w.shape[2], n_tiles) for w in ffn_weights]
    h, *ffn_w16 = pl.pallas_call(
        functools.partial(_mix_kernel, tm=tm, tiles_per_seq=seq // tm),
        grid=(n_tiles,),
        in_specs=[
            pl.BlockSpec((tm, D_MODEL), lambda n: (jnp.minimum(n + 1, n_tiles - 1), 0)),
            pl.BlockSpec((tm, D_MODEL), lambda n: (n, 0)),
            _const_spec((1, D_MODEL)),
            _const_spec((None, w_in.shape[2], w_in.shape[1])),
            _const_spec((None, GLA_GATE_RANK, GLA_KEY)),
            _const_spec((1, GLA_KEY)),
            _const_spec((TRIL_ROWS, TRIL_ROWS)),
            _const_spec((SUB_BLOCK * GLA_KEY, GLA_KEY)),
            _const_spec((1, GLA_DV)),
            _const_spec((None, len(POOL_WINDOWS), POOL_GC, POOL_GC)),
            _const_spec((1, POOL_WIDTH)),
            _const_spec((None, D_MODEL, D_MODEL)),
        ] + [src for src, _ in slabs],
        out_specs=[pl.BlockSpec((tm, D_MODEL), lambda n: (n, 0))] + [dst for _, dst in slabs],
        out_shape=[jax.ShapeDtypeStruct((n_tok, D_MODEL), F32)]
        + [jax.ShapeDtypeStruct(w.shape[1:], BF16) for w in ffn_weights],
        scratch_shapes=[
            pltpu.VMEM((D_MODEL, 2 * GLA_KEY + 2 * GLA_VAL + POOL_WIDTH), BF16),
            pltpu.VMEM((D_MODEL, GLA_KEY), BF16),
            pltpu.VMEM((D_MODEL, D_MODEL), BF16),
            pltpu.VMEM((POOL_WIDTH, D_MODEL), BF16),
            pltpu.VMEM((3 * KEY_TILES, tm, LANES), F32),
            pltpu.VMEM((tm, 2 * GLA_VAL), BF16),
            pltpu.VMEM((tm, POOL_WIDTH), F32),
            pltpu.VMEM((3 * KEY_TILES, tm, LANES), F32),
            pltpu.VMEM((tm, 2 * GLA_VAL), BF16),
            pltpu.VMEM((tm, POOL_WIDTH), F32),
            pltpu.VMEM((GLA_DV, GLA_KEY), F32),
            pltpu.VMEM((POOL_HALO + tm, POOL_WIDTH), F32),
            pltpu.VMEM((KEY_TILES, tm, LANES), F32),
            pltpu.VMEM((tm, SUBLANES * GLA_KEY), BF16),
            pltpu.VMEM((tm // 2, SUBLANES * GLA_KEY), BF16),
            pltpu.VMEM((tm, GLA_KEY), F32),
            pltpu.VMEM((tm, GLA_VAL), F32),
        ],
        compiler_params=pltpu.CompilerParams(
            dimension_semantics=("arbitrary",), vmem_limit_bytes=VMEM_LIMIT),
        name="mix",
    )(h, h, ln_mix, jnp.swapaxes(w_in, 1, 2), w_gate_lr, b_gate, _chunk_tril(),
      _selection_matrix(), gla_norm, w_pool, pool_scale, w_out, *ffn_weights)

    h = pl.pallas_call(
        _ffn_kernel,
        grid=(n_tok // TM_FFN,),
        in_specs=[
            pl.BlockSpec((TM_FFN, D_MODEL), lambda r: (r, 0)),
            pl.BlockSpec((TM_FFN, D_PLE), lambda r: (r, 0)),
            _const_spec((1, D_MODEL)),
            _const_spec((D_MODEL, D_FF)),
            _const_spec((D_MODEL, D_FF)),
            _const_spec((D_FF, D_MODEL)),
            _const_spec((1, D_MODEL)),
            _const_spec((D_MODEL, D_MODEL)),
            _const_spec((D_PLE, D_MODEL)),
            _const_spec((1, D_MODEL)),
        ],
        out_specs=pl.BlockSpec((TM_FFN, D_MODEL), lambda r: (r, 0)),
        out_shape=jax.ShapeDtypeStruct((n_tok, D_MODEL), F32),
        scratch_shapes=[pltpu.VMEM((TM_FFN, D_FF), BF16)],
        compiler_params=pltpu.CompilerParams(
            dimension_semantics=("arbitrary",), vmem_limit_bytes=VMEM_LIMIT),
        name="ffn",
    )(h, p.reshape(n_tok, D_PLE), ln_ffn, ffn_w16[0], ffn_w16[1], ffn_w16[2], ln_ple,
      ffn_w16[3], ffn_w16[4], ln_final[None])
    return h.reshape(bsz, seq, D_MODEL)


def kernel(x, p, ln_mix, w_in, w_gate_lr, b_gate, gla_norm, w_pool, pool_scale, w_out, ln_ffn,
           w_ffn_gate, w_ffn_up, w_ffn_down, ln_ple, w_ple_gate, w_ple_proj, ln_final):
    return _forward(x, p, ln_mix, w_in, w_gate_lr, b_gate, gla_norm, w_pool, pool_scale, w_out,
                    ln_ffn, w_ffn_gate, w_ffn_up, w_ffn_down, ln_ple, w_ple_gate, w_ple_proj,
                    ln_final)
```
